```python
import math
import jax, jax.numpy as jnp
from jax import lax
import numpy as np

D_MODEL = 2048
BATCH = 16
SEQ = 256
DEPTH = 2
DEC_BATCH = 8
DEC_SEQ = 4096
PAST_LEN = 256

GRID_W = 64
MIX_HEAD = 128
N_MIX_HEADS = D_MODEL // MIX_HEAD
H_RET = N_MIX_HEADS // 4
DK_RET = MIX_HEAD
DV_RET = MIX_HEAD
H_GQA = N_MIX_HEADS // 2
KV_GQA = H_GQA // 4
HD_GQA = MIX_HEAD
H_DIFF = N_MIX_HEADS // 4
DK_DIFF = MIX_HEAD // 2
DV_DIFF = MIX_HEAD
D_FF = 11 * D_MODEL // 4
N_MOD = 9
Q_BLOCK = 128
RET_CHUNK = 128
ROPE_THETA = 10000.0
EPS = 1e-6
IN_SIZES = (H_RET * DK_RET, H_RET * DK_RET, H_RET * DV_RET, H_RET * DV_RET,
            H_GQA * HD_GQA, KV_GQA * HD_GQA, KV_GQA * HD_GQA,
            H_DIFF * 2 * DK_DIFF, H_DIFF * 2 * DK_DIFF, H_DIFF * DV_DIFF)
IN_COLS = sum(IN_SIZES)

kernel_name = 'hybrid_retention_gqa_diffattn_macaron_dit_step'

F32 = jnp.float32


def rmsnorm(x, w):
    xf = x.astype(F32)
    y = xf * lax.rsqrt(jnp.mean(xf * xf, axis=-1, keepdims=True) + EPS)
    return (y * w.astype(F32)).astype(x.dtype)


def head_groupnorm(x, w):
    xf = x.astype(F32)
    xc = xf - jnp.mean(xf, axis=-1, keepdims=True)
    var = jnp.mean(xc * xc, axis=-1, keepdims=True)
    return xc * lax.rsqrt(var + EPS) * w.astype(F32)


def swiglu(h, w_in, w_out):
    gate, up = jnp.split(h @ w_in, 2, axis=-1)
    return (jax.nn.silu(gate) * up) @ w_out


def modulation(cvec, w, b):
    m = jax.nn.silu(cvec) @ w + b
    return m.reshape(-1, N_MOD, 1, D_MODEL)


def axial_rope(rows, d):
    axis = d // 2
    inv = 1.0 / (ROPE_THETA ** (jnp.arange(0, axis, 2, dtype=F32) / axis))
    r = jnp.repeat(jnp.arange(rows, dtype=F32), GRID_W)
    cl = jnp.tile(jnp.arange(GRID_W, dtype=F32), rows)
    ang = jnp.concatenate([r[:, None] * inv, cl[:, None] * inv], axis=-1)
    return jnp.cos(ang), jnp.sin(ang)


def apply_rope(x, cos, sin):
    xf = x.astype(F32)
    x1, x2 = xf[..., 0::2], xf[..., 1::2]
    shape = (x.shape[1],) + (1,) * (x.ndim - 3) + (cos.shape[-1],)
    c = cos.reshape(shape)
    s = sin.reshape(shape)
    out = jnp.stack([x1 * c - x2 * s, x1 * s + x2 * c], axis=-1).reshape(x.shape)
    return out.astype(x.dtype)


def retention_scan(q, k, v, log_gamma, state0):
    B, T, H, _ = q.shape
    dv = v.shape[-1]
    n = T // RET_CHUNK
    idx = jnp.arange(RET_CHUNK, dtype=F32)
    dif = idx[:, None] - idx[None, :]
    lg = log_gamma.astype(F32)
    inner_decay = jnp.where((dif >= 0)[None], jnp.exp(jnp.maximum(dif, 0.0)[None] * lg[:, None, None]), 0.0)
    q_decay = jnp.exp((idx + 1.0)[:, None] * lg[None])
    k_decay = jnp.exp((RET_CHUNK - 1.0 - idx)[:, None] * lg[None])
    chunk_decay = jnp.exp(RET_CHUNK * lg)

    def to_chunks(a):
        return a.astype(F32).reshape(B, n, RET_CHUNK, H, a.shape[-1]).swapaxes(0, 1)

    def step(S, inp):
        qc, kc, vc = inp
        att = jnp.einsum('bihd,bjhd->bhij', qc, kc) * inner_decay
        o = (jnp.einsum('bhij,bjhe->bihe', att, vc)
             + jnp.einsum('bihd,bhde->bihe', qc, S) * q_decay[None, :, :, None])
        S = S * chunk_decay[None, :, None, None] + jnp.einsum(
            'bjhd,bjhe->bhde', kc * k_decay[None, :, :, None], vc)
        return S, o

    S, o = lax.scan(step, state0.astype(F32), (to_chunks(q), to_chunks(k), to_chunks(v)))
    return o.swapaxes(0, 1).reshape(B, T, H, dv), S


def bidir_retention(q, k, v, log_gammas, s_fwd, s_bwd):
    o_f, S_f = retention_scan(q, k, v, log_gammas[0], s_fwd)
    o_b, S_b = retention_scan(q[:, ::-1], k[:, ::-1], v[:, ::-1], log_gammas[1], s_bwd)
    return o_f + o_b[:, ::-1], jnp.stack([S_f, S_b], axis=1)


def gqa_blocks(q, k, v, scale):
    B, T = q.shape[:2]
    nb = T // Q_BLOCK
    qb = q.reshape((B, nb, Q_BLOCK) + q.shape[2:]).swapaxes(0, 1)

    def one(qi):
        s = jnp.einsum('bqkgd,bskd->bkgqs', qi, k, preferred_element_type=F32) * scale
        p = jax.nn.softmax(s, axis=-1).astype(v.dtype)
        return jnp.einsum('bkgqs,bskd->bqkgd', p, v)

    out = lax.map(one, qb)
    return out.swapaxes(0, 1).reshape((B, T) + out.shape[3:])


def diff_blocks(q, k, v, lam, scale):
    B, T = q.shape[:2]
    nb = T // Q_BLOCK
    qb = q.reshape((B, nb, Q_BLOCK) + q.shape[2:]).swapaxes(0, 1)

    def one(qi):
        s = jnp.einsum('bqhcd,bshcd->bhcqs', qi, k, preferred_element_type=F32) * scale
        p = jax.nn.softmax(s, axis=-1)
        w = (p[:, :, 0] - lam * p[:, :, 1]).astype(v.dtype)
        return jnp.einsum('bhqs,bshe->bqhe', w, v)

    out = lax.map(one, qb)
    return out.swapaxes(0, 1).reshape((B, T) + out.shape[3:])


def token_mixer(h, lp, lam_init, rope, ctx):
    B, T, _ = h.shape
    split_points = np.cumsum(IN_SIZES)[:-1].tolist()
    rq, rk, rv, rg, gq, gk, gv, dq, dk, dv = jnp.split(h @ lp['w_in'], split_points, axis=-1)

    q_r = rq.reshape(B, T, H_RET, DK_RET)
    k_r = rk.reshape(B, T, H_RET, DK_RET) * (DK_RET ** -0.5)
    v_r = rv.reshape(B, T, H_RET, DV_RET)
    log_g = jax.nn.log_sigmoid(lp['ret_decay'].astype(F32))
    if ctx is None:
        s_f = jnp.zeros((B, H_RET, DK_RET, DV_RET), F32)
        s_b = s_f
    else:
        s_f = ctx['state_ret'][:, 0]
        s_b = ctx['state_ret'][:, 1]
    o_r, s_new = bidir_retention(q_r, k_r, v_r, log_g, s_f, s_b)
    ret_out = head_groupnorm(o_r, lp['ret_gn_w']).astype(h.dtype) * jax.nn.silu(rg).reshape(B, T, H_RET, DV_RET)

    q_g = rmsnorm(gq.reshape(B, T, H_GQA, HD_GQA), lp['gqa_qk_norm'][0])
    k_g = rmsnorm(gk.reshape(B, T, KV_GQA, HD_GQA), lp['gqa_qk_norm'][1])
    v_g = gv.reshape(B, T, KV_GQA, HD_GQA)

    q_d = dq.reshape(B, T, H_DIFF, 2, DK_DIFF)
    k_d = dk.reshape(B, T, H_DIFF, 2, DK_DIFF)
    v_d = dv.reshape(B, T, H_DIFF, DV_DIFF)

    if ctx is None:
        keys_g, vals_g, keys_d, vals_d = k_g, v_g, k_d, v_d
        new = (s_new, k_g, v_g, k_d.reshape(B, T, H_DIFF, 2 * DK_DIFF), v_d)
    else:
        (cos_g, sin_g), (cos_d, sin_d) = rope
        q_g = apply_rope(q_g, cos_g, sin_g)
        keys_g = jnp.concatenate([apply_rope(k_g, cos_g, sin_g), ctx['k_gqa'].astype(h.dtype)], axis=1)
        vals_g = jnp.concatenate([v_g, ctx['v_gqa'].astype(h.dtype)], axis=1)
        q_d = apply_rope(q_d, cos_d, sin_d)
        c_kd = ctx['k_diff'].astype(h.dtype).reshape(B, -1, H_DIFF, 2, DK_DIFF)
        keys_d = jnp.concatenate([apply_rope(k_d, cos_d, sin_d), c_kd], axis=1)
        vals_d = jnp.concatenate([v_d, ctx['v_diff'].astype(h.dtype)], axis=1)
        new = None

    gqa_out = gqa_blocks(q_g.reshape(B, T, KV_GQA, H_GQA // KV_GQA, HD_GQA), keys_g, vals_g,
                         HD_GQA ** -0.5).reshape(B, T, H_GQA * HD_GQA)

    dl = lp['diff_lambda'].astype(F32)
    lam = jnp.exp(jnp.sum(dl[0] * dl[1])) - jnp.exp(jnp.sum(dl[2] * dl[3])) + lam_init
    o_d = diff_blocks(q_d, keys_d, vals_d, lam, DK_DIFF ** -0.5)
    diff_out = rmsnorm(o_d, lp['diff_norm_w']) * (1.0 - lam_init)

    mixed = jnp.concatenate([ret_out.reshape(B, T, -1), gqa_out, diff_out.reshape(B, T, -1)], axis=-1)
    return mixed @ lp['w_out'], new


def trunk_layer(x, mod, lp, lam_init, rope, ctx):
    def modulated(x, i):
        h = rmsnorm(x, lp['norm_w'][i])
        return h * (1.0 + mod[:, 3 * i + 1]) + mod[:, 3 * i]

    x = x + 0.5 * mod[:, 2] * swiglu(modulated(x, 0), lp['ffn_w_in'][0], lp['ffn_w_out'][0])
    out, new = token_mixer(modulated(x, 1), lp, lam_init, rope, ctx)
    x = x + mod[:, 5] * out
    x = x + 0.5 * mod[:, 8] * swiglu(modulated(x, 2), lp['ffn_w_in'][1], lp['ffn_w_out'][1])
    return x, new


def setup_inputs(seed: int = 0) -> dict:
    key = jax.random.key(seed)
    ks = jax.random.split(key, 24)

    def nrm(k, shape, scale):
        return jax.random.normal(k, shape, F32) * scale

    g = 1.0 - 2.0 ** (-5.0 - np.arange(H_RET))
    decay_logit = jnp.asarray(np.log(g / (1.0 - g)), dtype=F32)
    return {
        'x_prompt': nrm(ks[0], (BATCH, SEQ, D_MODEL), 1.0),
        'x_sample': nrm(ks[1], (DEC_BATCH, DEC_SEQ, D_MODEL), 1.0),
        'c': nrm(ks[2], (DEC_BATCH, D_MODEL), 1.0),
        'state_ret': nrm(ks[3], (DEC_BATCH, DEPTH, 2, H_RET, DK_RET, DV_RET), 0.5),
        'cache_gqa_k': nrm(ks[4], (DEC_BATCH, DEPTH, PAST_LEN, KV_GQA, HD_GQA), 1.0),
        'cache_gqa_v': nrm(ks[5], (DEC_BATCH, DEPTH, PAST_LEN, KV_GQA, HD_GQA), 1.0),
        'cache_diff_k': nrm(ks[6], (DEC_BATCH, DEPTH, PAST_LEN, H_DIFF, 2 * DK_DIFF), 1.0),
        'cache_diff_v': nrm(ks[7], (DEC_BATCH, DEPTH, PAST_LEN, H_DIFF, DV_DIFF), 1.0),
        'c_ctx': nrm(ks[8], (D_MODEL,), 1.0),
        'w_mod': nrm(ks[9], (DEPTH, D_MODEL, N_MOD * D_MODEL), 0.5 * D_MODEL ** -0.5),
        'b_mod': nrm(ks[10], (DEPTH, N_MOD * D_MODEL), 0.01),
        'norm_w': 1.0 + nrm(ks[11], (DEPTH, 3, D_MODEL), 0.02),
        'ffn_w_in': nrm(ks[12], (DEPTH, 2, D_MODEL, 2 * D_FF), D_MODEL ** -0.5),
        'ffn_w_out': nrm(ks[13], (DEPTH, 2, D_FF, D_MODEL), D_FF ** -0.5),
        'w_in': nrm(ks[14], (DEPTH, D_MODEL, IN_COLS), D_MODEL ** -0.5),
        'w_out': nrm(ks[15], (DEPTH, D_MODEL, D_MODEL), D_MODEL ** -0.5),
        'ret_decay': decay_logit[None, None, :] + nrm(ks[16], (DEPTH, 2, H_RET), 0.1),
        'ret_gn_w': 1.0 + nrm(ks[17], (DEPTH, H_RET, DV_RET), 0.02),
        'gqa_qk_norm': 1.0 + nrm(ks[18], (DEPTH, 2, HD_GQA), 0.02),
        'diff_lambda': nrm(ks[19], (DEPTH, 4, DK_DIFF), 0.1),
        'diff_norm_w': 1.0 + nrm(ks[20], (DEPTH, H_DIFF, DV_DIFF), 0.02),
        'final_norm_w': 1.0 + nrm(ks[21], (D_MODEL,), 0.02),
    }


def reference(x_prompt, x_sample, c, state_ret, cache_gqa_k, cache_gqa_v, cache_diff_k, cache_diff_v,
              c_ctx, w_mod, b_mod, norm_w, ffn_w_in, ffn_w_out, w_in, w_out, ret_decay, ret_gn_w,
              gqa_qk_norm, diff_lambda, diff_norm_w, final_norm_w):
    rows = x_sample.shape[1] // GRID_W
    rope = (axial_rope(rows, HD_GQA), axial_rope(rows, DK_DIFF))
    xp, xs = x_prompt, x_sample
    new_ret, new_gk, new_gv, new_dk, new_dv = [], [], [], [], []
    for l in range(DEPTH):
        lp = {'w_in': w_in[l], 'w_out': w_out[l], 'norm_w': norm_w[l],
              'ffn_w_in': ffn_w_in[l], 'ffn_w_out': ffn_w_out[l],
              'ret_decay': ret_decay[l], 'ret_gn_w': ret_gn_w[l], 'gqa_qk_norm': gqa_qk_norm[l],
              'diff_lambda': diff_lambda[l], 'diff_norm_w': diff_norm_w[l]}
        lam_init = 0.8 - 0.6 * math.exp(-0.3 * l)
        xp, new = trunk_layer(xp, modulation(c_ctx, w_mod[l], b_mod[l]), lp, lam_init, None, None)
        new_ret.append(new[0].astype(x_prompt.dtype))
        new_gk.append(new[1])
        new_gv.append(new[2])
        new_dk.append(new[3])
        new_dv.append(new[4])
        ctx_l = {'state_ret': state_ret[:, l], 'k_gqa': cache_gqa_k[:, l], 'v_gqa': cache_gqa_v[:, l],
                 'k_diff': cache_diff_k[:, l], 'v_diff': cache_diff_v[:, l]}
        xs, _ = trunk_layer(xs, modulation(c, w_mod[l], b_mod[l]), lp, lam_init, rope, ctx_l)
    y_prompt = rmsnorm(xp, final_norm_w)
    y_sample = rmsnorm(xs, final_norm_w)
    new_state_ret = jnp.stack(new_ret, axis=1)
    new_gqa_k = jnp.stack(new_gk, axis=1)
    new_gqa_v = jnp.stack(new_gv, axis=1)
    new_diff_k = jnp.stack(new_dk, axis=1)
    new_diff_v = jnp.stack(new_dv, axis=1)
    return (y_prompt, y_sample, new_state_ret, new_gqa_k, new_gqa_v, new_diff_k, new_diff_v)
```

```python
import functools
import math

import jax
import jax.numpy as jnp
from jax import lax
from jax.experimental import pallas as pl
from jax.experimental.pallas import tpu as pltpu

F32 = jnp.float32
BF16 = jnp.bfloat16

D_MODEL = 2048
GRID_W = 64
HEAD = 128
H_RET = 4
H_GQA = 8
KV_GQA = 2
GQA_GROUP = H_GQA // KV_GQA
H_DIFF = 4
DK_DIFF = HEAD // 2
D_FF = 11 * D_MODEL // 4
N_MOD = 9
RET_CHUNK = 128
ROPE_THETA = 10000.0
EPS = 1e-6
IN_COLS = 5120

COL_RQ, COL_RK, COL_RV, COL_RG = 0, 4, 8, 12
COL_GQ, COL_GK, COL_GV = 16, 24, 26
COL_DQ, COL_DK, COL_DV = 28, 32, 36

VMEM_LIMIT = 56 * 1024 * 1024
ROW_TILE = 1024


def _params(sem):
    return pltpu.CompilerParams(dimension_semantics=sem, vmem_limit_bytes=VMEM_LIMIT)


def _mod_kernel(c_ref, w_ref, b_ref, o_ref):
    c = c_ref[...]
    a = (c * jax.nn.sigmoid(c)).astype(BF16)
    o_ref[...] = jnp.dot(a, w_ref[...].astype(BF16), preferred_element_type=F32) + b_ref[...]


def _modulation(cond, w, b):
    r = cond.shape[0]
    n = w.shape[1]
    tn = 1024
    return pl.pallas_call(
        _mod_kernel,
        grid=(n // tn,),
        in_specs=[pl.BlockSpec((r, D_MODEL), lambda j: (0, 0)),
                  pl.BlockSpec((D_MODEL, tn), lambda j: (0, j)),
                  pl.BlockSpec((1, tn), lambda j: (0, j))],
        out_specs=pl.BlockSpec((r, tn), lambda j: (0, j)),
        out_shape=jax.ShapeDtypeStruct((r, n), F32),
        compiler_params=_params(("arbitrary",)),
        name="modulation",
    )(cond, w, b)


def _modulated_norm(x, nw_ref, mod_ref, sub):
    ms = jnp.mean(x * x, axis=-1, keepdims=True)
    y = x * lax.rsqrt(ms + EPS) * nw_ref[sub:sub + 1, :]
    return y * (1.0 + mod_ref[0, 3 * sub + 1:3 * sub + 2, :]) + mod_ref[0, 3 * sub:3 * sub + 1, :]


def _rope(x, cos, sin_signed):
    n = x.shape[-1]
    nxt = pltpu.roll(x, n - 1, axis=1)
    prv = pltpu.roll(x, 1, axis=1)
    lane = lax.broadcasted_iota(jnp.int32, x.shape, 1)
    return x * cos + jnp.where((lane & 1) == 0, nxt, prv) * sin_signed


def _ffn_up_kernel(x_ref, mod_ref, nw_ref, wg_ref, wu_ref, o_ref, h_scr, *, sub):
    @pl.when(pl.program_id(1) == 0)
    def _():
        h_scr[...] = _modulated_norm(x_ref[...], nw_ref, mod_ref, sub).astype(BF16)

    h = h_scr[...]
    g = jnp.dot(h, wg_ref[...], preferred_element_type=F32)
    u = jnp.dot(h, wu_ref[...], preferred_element_type=F32)
    o_ref[...] = (g * jax.nn.sigmoid(g) * u).astype(BF16)


def _ffn_up(x, mod, nw, w_in, *, sub, rows_per_mod, tm, tn):
    m = x.shape[0]
    nj = D_FF // tn
    bpm = rows_per_mod // tm
    return pl.pallas_call(
        functools.partial(_ffn_up_kernel, sub=sub),
        grid=(m // tm, nj),
        in_specs=[pl.BlockSpec((tm, D_MODEL), lambda i, j: (i, 0)),
                  pl.BlockSpec((1, N_MOD, D_MODEL), lambda i, j: (i // bpm, 0, 0)),
                  pl.BlockSpec((3, D_MODEL), lambda i, j: (0, 0)),
                  pl.BlockSpec((D_MODEL, tn), lambda i, j: (0, j)),
                  pl.BlockSpec((D_MODEL, tn), lambda i, j: (0, j + nj))],
        out_specs=pl.BlockSpec((tm, tn), lambda i, j: (i, j)),
        out_shape=jax.ShapeDtypeStruct((m, D_FF), BF16),
        scratch_shapes=[pltpu.VMEM((tm, D_MODEL), BF16)],
        compiler_params=_params(("parallel", "arbitrary")),
        name="ffn_up",
    )(x, mod, nw, w_in, w_in)


def _ffn_down_kernel(h_ref, w_ref, x_ref, mod_ref, o_ref, *, gate_row):
    acc = jnp.dot(h_ref[...], w_ref[...], preferred_element_type=F32)
    o_ref[...] = x_ref[...] + (0.5 * mod_ref[0, gate_row:gate_row + 1, :]) * acc


def _ffn_down(h, w_out, x, mod, *, gate_row, rows_per_mod, tm, tn):
    m = x.shape[0]
    bpm = rows_per_mod // tm
    return pl.pallas_call(
        functools.partial(_ffn_down_kernel, gate_row=gate_row),
        grid=(m // tm, D_MODEL // tn),
        in_specs=[pl.BlockSpec((tm, D_FF), lambda i, j: (i, 0)),
                  pl.BlockSpec((D_FF, tn), lambda i, j: (0, j)),
                  pl.BlockSpec((tm, tn), lambda i, j: (i, j)),
                  pl.BlockSpec((1, N_MOD, tn), lambda i, j: (i // bpm, 0, j))],
        out_specs=pl.BlockSpec((tm, tn), lambda i, j: (i, j)),
        out_shape=jax.ShapeDtypeStruct((m, D_MODEL), F32),
        compiler_params=_params(("parallel", "arbitrary")),
        name="ffn_down",
    )(h, w_out, x, mod)


def _in_proj_kernel(x_ref, mod_ref, nw_ref, w_ref, o_ref, h_scr):
    @pl.when(pl.program_id(1) == 0)
    def _():
        h_scr[...] = _modulated_norm(x_ref[...], nw_ref, mod_ref, 1).astype(BF16)

    o_ref[...] = jnp.dot(h_scr[...], w_ref[...], preferred_element_type=F32)


def _in_proj(x, mod, nw, w_in, *, rows_per_mod, tm, tn):
    m = x.shape[0]
    bpm = rows_per_mod // tm
    return pl.pallas_call(
        _in_proj_kernel,
        grid=(m // tm, IN_COLS // tn),
        in_specs=[pl.BlockSpec((tm, D_MODEL), lambda i, j: (i, 0)),
                  pl.BlockSpec((1, N_MOD, D_MODEL), lambda i, j: (i // bpm, 0, 0)),
                  pl.BlockSpec((3, D_MODEL), lambda i, j: (0, 0)),
                  pl.BlockSpec((D_MODEL, tn), lambda i, j: (0, j))],
        out_specs=pl.BlockSpec((tm, tn), lambda i, j: (i, j)),
        out_shape=jax.ShapeDtypeStruct((m, IN_COLS), F32),
        scratch_shapes=[pltpu.VMEM((tm, D_MODEL), BF16)],
        compiler_params=_params(("parallel", "arbitrary")),
        name="mixer_in_proj",
    )(x, mod, nw, w_in)


def _out_proj_kernel(r_ref, g_ref, d_ref, w_ref, x_ref, mod_ref, o_ref):
    mixed = jnp.concatenate([r_ref[...], g_ref[...], d_ref[...]], axis=1)
    acc = jnp.dot(mixed, w_ref[...], preferred_element_type=F32)
    o_ref[...] = x_ref[...] + mod_ref[0, 5:6, :] * acc


def _out_proj(ret, gqa, dif, w_out, x, mod, *, rows_per_mod, tm):
    m = x.shape[0]
    bpm = rows_per_mod // tm
    return pl.pallas_call(
        _out_proj_kernel,
        grid=(m // tm,),
        in_specs=[pl.BlockSpec((tm, H_RET * HEAD), lambda i: (i, 0)),
                  pl.BlockSpec((tm, H_GQA * HEAD), lambda i: (i, 0)),
                  pl.BlockSpec((tm, H_DIFF * HEAD), lambda i: (i, 0)),
                  pl.BlockSpec((D_MODEL, D_MODEL), lambda i: (0, 0)),
                  pl.BlockSpec((tm, D_MODEL), lambda i: (i, 0)),
                  pl.BlockSpec((1, N_MOD, D_MODEL), lambda i: (i // bpm, 0, 0))],
        out_specs=pl.BlockSpec((tm, D_MODEL), lambda i: (i, 0)),
        out_shape=jax.ShapeDtypeStruct((m, D_MODEL), F32),
        compiler_params=_params(("parallel",)),
        name="mixer_out_proj",
    )(ret, gqa, dif, w_out, x, mod)


def _retention_kernel(*refs, seq, has_state):
    if has_state:
        q_ref, k_ref, v_ref, g_ref, dec_ref, gn_ref, st_ref, o_ref, ns_ref, of_scr = refs
    else:
        q_ref, k_ref, v_ref, g_ref, dec_ref, gn_ref, o_ref, ns_ref, of_scr = refs
        st_ref = None
    c = RET_CHUNK
    n = seq // c
    row = lax.broadcasted_iota(jnp.int32, (c, c), 0)
    col = lax.broadcasted_iota(jnp.int32, (c, c), 1)
    ridx = lax.broadcasted_iota(jnp.int32, (c, 1), 0).astype(F32)

    def log_gamma(d):
        z = dec_ref[d]
        return jnp.minimum(z, 0.0) - jnp.log(1.0 + jnp.exp(-jnp.abs(z)))

    def chunk(ci, s, lg, inner, q_dec, k_dec, c_dec):
        rows = pl.ds(pl.multiple_of(ci * c, c), c)
        qc = q_ref[rows, :]
        kc = k_ref[rows, :] * (HEAD ** -0.5)
        vc = v_ref[rows, :].astype(BF16)
        qb = qc.astype(BF16)
        att = lax.dot_general(qb, kc.astype(BF16), (((1,), (1,)), ((), ())),
                              preferred_element_type=F32) * inner
        o = (jnp.dot(att.astype(BF16), vc, preferred_element_type=F32)
             + jnp.dot(qb, s.astype(BF16), preferred_element_type=F32) * q_dec)
        kd = (kc * k_dec).astype(BF16)
        s_new = s * c_dec + lax.dot_general(kd, vc, (((0,), (0,)), ((), ())),
                                            preferred_element_type=F32)
        return rows, o, s_new

    lg = log_gamma(0)
    dif = (row - col).astype(F32)
    inner = jnp.where(row >= col, jnp.exp(jnp.maximum(dif, 0.0) * lg), 0.0)
    q_dec = jnp.exp((ridx + 1.0) * lg)
    k_dec = jnp.exp((c - 1.0 - ridx) * lg)
    c_dec = jnp.exp(float(c) * lg)
    s0 = st_ref[0] if has_state else jnp.zeros((HEAD, HEAD), F32)

    def fwd(ci, s):
        rows, o, s_new = chunk(ci, s, lg, inner, q_dec, k_dec, c_dec)
        of_scr[rows, :] = o
        return s_new

    ns_ref[0] = lax.fori_loop(0, n, fwd, s0)

    lg = log_gamma(1)
    dif = (col - row).astype(F32)
    inner = jnp.where(col >= row, jnp.exp(jnp.maximum(dif, 0.0) * lg), 0.0)
    q_dec = jnp.exp((float(c) - ridx) * lg)
    k_dec = jnp.exp(ridx * lg)
    c_dec = jnp.exp(float(c) * lg)
    s0 = st_ref[1] if has_state else jnp.zeros((HEAD, HEAD), F32)

    def bwd(t, s):
        rows, o, s_new = chunk(n - 1 - t, s, lg, inner, q_dec, k_dec, c_dec)
        o = o + of_scr[rows, :]
        xc = o - jnp.mean(o, axis=-1, keepdims=True)
        var = jnp.mean(xc * xc, axis=-1, keepdims=True)
        gate = g_ref[rows, :]
        y = (xc * lax.rsqrt(var + EPS) * gn_ref[...]) * (gate * jax.nn.sigmoid(gate))
        o_ref[rows, :] = y.astype(BF16)
        return s_new

    ns_ref[1] = lax.fori_loop(0, n, bwd, s0)


def _retention(proj, dec, gn_w, state, *, batch, seq):
    has_state = state is not None

    def col(off):
        return pl.BlockSpec((seq, HEAD), lambda b, h: (b, off + h))

    in_specs = [col(COL_RQ), col(COL_RK), col(COL_RV), col(COL_RG),
                pl.BlockSpec((2, None, 1, 1), lambda b, h: (0, h, 0, 0)),
                pl.BlockSpec((None, 1, HEAD), lambda b, h: (h, 0, 0))]
    args = [proj, proj, proj, proj, dec, gn_w]
    if has_state:
        in_specs.append(pl.BlockSpec((None, 2, None, HEAD, HEAD), lambda b, h: (b, 0, h, 0, 0)))
        args.append(state)
    out, new_state = pl.pallas_call(
        functools.partial(_retention_kernel, seq=seq, has_state=has_state),
        grid=(batch, H_RET),
        in_specs=in_specs,
        out_specs=[pl.BlockSpec((seq, HEAD), lambda b, h: (b, h)),
                   pl.BlockSpec((None, 2, None, HEAD, HEAD), lambda b, h: (b, 0, h, 0, 0))],
        out_shape=[jax.ShapeDtypeStruct((batch * seq, H_RET * HEAD), BF16),
                   jax.ShapeDtypeStruct((batch, 2, H_RET, HEAD, HEAD), F32)],
        scratch_shapes=[pltpu.VMEM((seq, HEAD), F32)],
        compiler_params=_params(("parallel", "parallel")),
        name="retention",
    )(*args)
    return out, new_state


def _gqa_kernel(*refs, seq, past, tq, latent):
    if latent:
        (q_ref, k_ref, v_ref, nw_ref, cos_ref, sin_ref, ck_ref, cv_ref,
         o_ref, k_scr, v_scr) = refs
    else:
        q_ref, k_ref, v_ref, nw_ref, o_ref, kn_ref, k_scr, v_scr = refs
    i = pl.program_id(2)

    def head_norm(x, w):
        return x * lax.rsqrt(jnp.mean(x * x, axis=-1, keepdims=True) + EPS) * w

    @pl.when(i == 0)
    def _():
        kn = head_norm(k_ref[...], nw_ref[1:2, :])
        if latent:
            k_scr[0:seq, :] = _rope(kn, cos_ref[...], sin_ref[...]).astype(BF16)
            k_scr[seq:seq + past, :] = ck_ref[...].astype(BF16)
            v_scr[0:seq, :] = v_ref[...].astype(BF16)
            v_scr[seq:seq + past, :] = cv_ref[...].astype(BF16)
        else:
            kn_ref[...] = kn
            k_scr[...] = kn.astype(BF16)
            v_scr[...] = v_ref[...].astype(BF16)

    q = q_ref[...]
    if latent:
        rows = pl.ds(pl.multiple_of(i * tq, tq), tq)
        cos = cos_ref[rows, :]
        sin = sin_ref[rows, :]
    heads = []
    for g in range(GQA_GROUP):
        qh = head_norm(q[:, g * HEAD:(g + 1) * HEAD], nw_ref[0:1, :])
        if latent:
            qh = _rope(qh, cos, sin)
        heads.append(qh.astype(BF16))
    qq = jnp.concatenate(heads, axis=0)
    s = lax.dot_general(qq, k_scr[...], (((1,), (1,)), ((), ())),
                        preferred_element_type=F32) * (HEAD ** -0.5)
    p = jnp.exp(s - jnp.max(s, axis=-1, keepdims=True))
    l = jnp.sum(p, axis=-1, keepdims=True)
    o = jnp.dot(p.astype(BF16), v_scr[...], preferred_element_type=F32) / l
    for g in range(GQA_GROUP):
        o_ref[:, g * HEAD:(g + 1) * HEAD] = o[g * tq:(g + 1) * tq, :].astype(BF16)


def _gqa(proj, qk_norm, rope, cache_k, cache_v, layer, *, batch, seq, tq):
    latent = rope is not None
    nq = seq // tq
    past = cache_k.shape[2] if latent else 0
    in_specs = [pl.BlockSpec((tq, GQA_GROUP * HEAD), lambda b, k, i: (b * nq + i, COL_GQ // GQA_GROUP + k)),
                pl.BlockSpec((seq, HEAD), lambda b, k, i: (b, COL_GK + k)),
                pl.BlockSpec((seq, HEAD), lambda b, k, i: (b, COL_GV + k)),
                pl.BlockSpec((2, HEAD), lambda b, k, i: (0, 0))]
    args = [proj, proj, proj, qk_norm]
    out_specs = [pl.BlockSpec((tq, GQA_GROUP * HEAD), lambda b, k, i: (b * nq + i, k))]
    out_shape = [jax.ShapeDtypeStruct((batch * seq, H_GQA * HEAD), BF16)]
    if latent:
        in_specs += [pl.BlockSpec((seq, HEAD), lambda b, k, i: (0, 0)),
                     pl.BlockSpec((seq, HEAD), lambda b, k, i: (0, 0)),
                     pl.BlockSpec((None, None, past, HEAD), lambda b, k, i: (b, layer, 0, k)),
                     pl.BlockSpec((None, None, past, HEAD), lambda b, k, i: (b, layer, 0, k))]
        args += [rope[0], rope[1], cache_k, cache_v]
    else:
        out_specs.append(pl.BlockSpec((seq, HEAD), lambda b, k, i: (b, k)))
        out_shape.append(jax.ShapeDtypeStruct((batch * seq, KV_GQA * HEAD), F32))
    res = pl.pallas_call(
        functools.partial(_gqa_kernel, seq=seq, past=past, tq=tq, latent=latent),
        grid=(batch, KV_GQA, nq),
        in_specs=in_specs,
        out_specs=out_specs,
        out_shape=out_shape,
        scratch_shapes=[pltpu.VMEM((seq + past, HEAD), BF16), pltpu.VMEM((seq + past, HEAD), BF16)],
        compiler_params=_params(("parallel", "parallel", "arbitrary")),
        name="gqa_attention",
    )(*args)
    return res if latent else (res[0], res[1])


def _diff_kernel(*refs, seq, past, tq, latent, lam_init):
    if latent:
        (q_ref, k_ref, v_ref, lam_ref, nw_ref, cos_ref, sin_ref, ck_ref, cv_ref,
         o_ref, k_scr, v_scr) = refs
    else:
        q_ref, k_ref, v_ref, lam_ref, nw_ref, o_ref, k_scr, v_scr = refs
    i = pl.program_id(2)

    @pl.when(i == 0)
    def _():
        if latent:
            k_scr[0:seq, :] = _rope(k_ref[...], cos_ref[...], sin_ref[...]).astype(BF16)
            k_scr[seq:seq + past, :] = ck_ref[...].astype(BF16)
            v_scr[0:seq, :] = v_ref[...].astype(BF16)
            v_scr[seq:seq + past, :] = cv_ref[...].astype(BF16)
        else:
            k_scr[...] = k_ref[...].astype(BF16)
            v_scr[...] = v_ref[...].astype(BF16)

    dl = lam_ref[...]
    lam = (jnp.exp(jnp.sum(dl[0:1, :] * dl[1:2, :], axis=-1, keepdims=True))
           - jnp.exp(jnp.sum(dl[2:3, :] * dl[3:4, :], axis=-1, keepdims=True)) + lam_init)

    q = q_ref[...]
    if latent:
        rows = pl.ds(pl.multiple_of(i * tq, tq), tq)
        q = _rope(q, cos_ref[rows, :], sin_ref[rows, :])
    lane = lax.broadcasted_iota(jnp.int32, q.shape, 1)
    q1 = jnp.where(lane < DK_DIFF, q, 0.0).astype(BF16)
    q2 = jnp.where(lane >= DK_DIFF, q, 0.0).astype(BF16)
    qq = jnp.concatenate([q1, q2], axis=0)
    s = lax.dot_general(qq, k_scr[...], (((1,), (1,)), ((), ())),
                        preferred_element_type=F32) * (DK_DIFF ** -0.5)
    e = jnp.exp(s - jnp.max(s, axis=-1, keepdims=True))
    r = 1.0 / jnp.sum(e, axis=-1, keepdims=True)
    w = e[0:tq, :] * r[0:tq, :] - lam * (e[tq:2 * tq, :] * r[tq:2 * tq, :])
    o = jnp.dot(w.astype(BF16), v_scr[...], preferred_element_type=F32)
    y = o * lax.rsqrt(jnp.mean(o * o, axis=-1, keepdims=True) + EPS) * nw_ref[...]
    o_ref[...] = (y * (1.0 - lam_init)).astype(BF16)


def _diff(proj, diff_lambda, norm_w, rope, cache_k, cache_v, layer, lam_init, *, batch, seq, tq):
    latent = rope is not None
    nq = seq // tq
    past = cache_k.shape[2] if latent else 0
    in_specs = [pl.BlockSpec((tq, HEAD), lambda b, h, i: (b * nq + i, COL_DQ + h)),
                pl.BlockSpec((seq, HEAD), lambda b, h, i: (b, COL_DK + h)),
                pl.BlockSpec((seq, HEAD), lambda b, h, i: (b, COL_DV + h)),
                pl.BlockSpec((4, DK_DIFF), lambda b, h, i: (0, 0)),
                pl.BlockSpec((None, 1, HEAD), lambda b, h, i: (h, 0, 0))]
    args = [proj, proj, proj, diff_lambda, norm_w]
    if latent:
        in_specs += [pl.BlockSpec((seq, HEAD), lambda b, h, i: (0, 0)),
                     pl.BlockSpec((seq, HEAD), lambda b, h, i: (0, 0)),
                     pl.BlockSpec((None, None, past, HEAD), lambda b, h, i: (b, layer, 0, h)),
                     pl.BlockSpec((None, None, past, HEAD), lambda b, h, i: (b, layer, 0, h))]
        args += [rope[0], rope[1], cache_k, cache_v]
    return pl.pallas_call(
        functools.partial(_diff_kernel, seq=seq, past=past, tq=tq, latent=latent, lam_init=lam_init),
        grid=(batch, H_DIFF, nq),
        in_specs=in_specs,
        out_specs=pl.BlockSpec((tq, HEAD), lambda b, h, i: (b * nq + i, h)),
        out_shape=jax.ShapeDtypeStruct((batch * seq, H_DIFF * HEAD), BF16),
        scratch_shapes=[pltpu.VMEM((seq + past, HEAD), BF16), pltpu.VMEM((seq + past, HEAD), BF16)],
        compiler_params=_params(("parallel", "parallel", "arbitrary")),
        name="diff_attention",
    )(*args)


def _final_norm_kernel(x_ref, w_ref, o_ref):
    x = x_ref[...]
    o_ref[...] = x * lax.rsqrt(jnp.mean(x * x, axis=-1, keepdims=True) + EPS) * w_ref[...]


def _final_norm(x, w, *, tm):
    m = x.shape[0]
    return pl.pallas_call(
        _final_norm_kernel,
        grid=(m // tm,),
        in_specs=[pl.BlockSpec((tm, D_MODEL), lambda i: (i, 0)),
                  pl.BlockSpec((1, D_MODEL), lambda i: (0, 0))],
        out_specs=pl.BlockSpec((tm, D_MODEL), lambda i: (i, 0)),
        out_shape=jax.ShapeDtypeStruct((m, D_MODEL), F32),
        compiler_params=_params(("parallel",)),
        name="final_norm",
    )(x, w)


def _rope_tables(rows, d, width):
    axis = d // 2
    inv = 1.0 / (ROPE_THETA ** (jnp.arange(0, axis, 2, dtype=F32) / axis))
    r = jnp.repeat(jnp.arange(rows, dtype=F32), GRID_W)
    cl = jnp.tile(jnp.arange(GRID_W, dtype=F32), rows)
    ang = jnp.concatenate([r[:, None] * inv, cl[:, None] * inv], axis=-1)
    cos = jnp.repeat(jnp.cos(ang), 2, axis=-1)
    sin = jnp.repeat(jnp.sin(ang), 2, axis=-1) * jnp.tile(jnp.asarray([-1.0, 1.0], F32), d // 2)
    reps = width // d
    return jnp.tile(cos, (1, reps)), jnp.tile(sin, (1, reps))


def _trunk_layer(x, mod, lw, *, batch, seq, tm, tq, lam_init, layer, ctx):
    rows_per_mod = seq if mod.shape[0] > 1 else batch * seq
    kw = dict(rows_per_mod=rows_per_mod, tm=tm)
    h = _ffn_up(x, mod, lw['norm_w'], lw['ffn_w_in'][0], sub=0, tn=512, **kw)
    x = _ffn_down(h, lw['ffn_w_out'][0], x, mod, gate_row=2, tn=512, **kw)

    proj = _in_proj(x, mod, lw['norm_w'], lw['w_in'], tn=1024, **kw)
    if ctx is None:
        ret, new_state = _retention(proj, lw['ret_decay'], lw['ret_gn_w'], None, batch=batch, seq=seq)
        gqa, k_norm = _gqa(proj, lw['gqa_qk_norm'], None, None, None, layer, batch=batch, seq=seq, tq=tq)
        dif = _diff(proj, lw['diff_lambda'], lw['diff_norm_w'], None, None, None, layer, lam_init,
                    batch=batch, seq=seq, tq=tq)
        new = (new_state,
               k_norm.reshape(batch, seq, KV_GQA, HEAD),
               proj[:, COL_GV * HEAD:(COL_GV + KV_GQA) * HEAD].reshape(batch, seq, KV_GQA, HEAD),
               proj[:, COL_DK * HEAD:(COL_DK + H_DIFF) * HEAD].reshape(batch, seq, H_DIFF, HEAD),
               proj[:, COL_DV * HEAD:(COL_DV + H_DIFF) * HEAD].reshape(batch, seq, H_DIFF, HEAD))
    else:
        ret, _ = _retention(proj, lw['ret_decay'], lw['ret_gn_w'], ctx['state_ret'], batch=batch, seq=seq)
        gqa = _gqa(proj, lw['gqa_qk_norm'], ctx['rope_g'], ctx['k_gqa'], ctx['v_gqa'], layer,
                   batch=batch, seq=seq, tq=tq)[0]
        dif = _diff(proj, lw['diff_lambda'], lw['diff_norm_w'], ctx['rope_d'], ctx['k_diff'], ctx['v_diff'],
                    layer, lam_init, batch=batch, seq=seq, tq=tq)
        new = None
    x = _out_proj(ret, gqa, dif, lw['w_out'], x, mod, rows_per_mod=rows_per_mod, tm=min(512, tm))

    h = _ffn_up(x, mod, lw['norm_w'], lw['ffn_w_in'][1], sub=2, tn=512, **kw)
    x = _ffn_down(h, lw['ffn_w_out'][1], x, mod, gate_row=8, tn=512, **kw)
    return x, new


def kernel(x_prompt, x_sample, c, state_ret, cache_gqa_k, cache_gqa_v, cache_diff_k, cache_diff_v, c_ctx,
           w_mod, b_mod, norm_w, ffn_w_in, ffn_w_out, w_in, w_out, ret_decay, ret_gn_w, gqa_qk_norm,
           diff_lambda, diff_norm_w, final_norm_w):
    bp, sp, _ = x_prompt.shape
    bs, ss, _ = x_sample.shape
    depth = w_in.shape[0]
    past = cache_gqa_k.shape[2]
    rows = ss // GRID_W
    rope_g = _rope_tables(rows, HEAD, HEAD)
    rope_d = _rope_tables(rows, DK_DIFF, HEAD)

    n_cond = 16
    cond = jnp.zeros((n_cond, D_MODEL), F32).at[0].set(c_ctx).at[1:1 + bs].set(c)

    ck_g = cache_gqa_k.reshape(bs, depth, past, KV_GQA * HEAD)
    cv_g = cache_gqa_v.reshape(bs, depth, past, KV_GQA * HEAD)
    ck_d = cache_diff_k.reshape(bs, depth, past, H_DIFF * HEAD)
    cv_d = cache_diff_v.reshape(bs, depth, past, H_DIFF * HEAD)

    xp = x_prompt.reshape(bp * sp, D_MODEL)
    xs = x_sample.reshape(bs * ss, D_MODEL)
    new_ret, new_gk, new_gv, new_dk, new_dv = [], [], [], [], []
    for l in range(depth):
        lam_init = 0.8 - 0.6 * math.exp(-0.3 * l)
        m = _modulation(cond, w_mod[l], b_mod[l].reshape(1, -1))
        mod_p = m[0:1].reshape(1, N_MOD, D_MODEL)
        mod_s = m[1:1 + bs].reshape(bs, N_MOD, D_MODEL)
        lw = {'norm_w': norm_w[l],
              'ffn_w_in': ffn_w_in[l].astype(BF16), 'ffn_w_out': ffn_w_out[l].astype(BF16),
              'w_in': w_in[l].astype(BF16), 'w_out': w_out[l].astype(BF16),
              'ret_decay': ret_decay[l].reshape(2, H_RET, 1, 1),
              'ret_gn_w': ret_gn_w[l].reshape(H_RET, 1, HEAD),
              'gqa_qk_norm': gqa_qk_norm[l],
              'diff_lambda': diff_lambda[l],
              'diff_norm_w': diff_norm_w[l].reshape(H_DIFF, 1, HEAD)}
        xp, new = _trunk_layer(xp, mod_p, lw, batch=bp, seq=sp, tm=min(ROW_TILE, bp * sp), tq=sp,
                               lam_init=lam_init, layer=l, ctx=None)
        new_ret.append(new[0])
        new_gk.append(new[1])
        new_gv.append(new[2])
        new_dk.append(new[3])
        new_dv.append(new[4])
        ctx = {'state_ret': state_ret[:, l], 'k_gqa': ck_g, 'v_gqa': cv_g, 'k_diff': ck_d, 'v_diff': cv_d,
               'rope_g': rope_g, 'rope_d': rope_d}
        xs, _ = _trunk_layer(xs, mod_s, lw, batch=bs, seq=ss, tm=min(ROW_TILE, ss), tq=128,
                             lam_init=lam_init, layer=l, ctx=ctx)
    fw = final_norm_w.reshape(1, D_MODEL)
    y_prompt = _final_norm(xp, fw, tm=min(ROW_TILE, bp * sp)).reshape(bp, sp, D_MODEL)
    y_sample = _final_norm(xs, fw, tm=min(ROW_TILE, ss)).reshape(bs, ss, D_MODEL)
    return (y_prompt, y_sample, jnp.stack(new_ret, axis=1), jnp.stack(new_gk, axis=1),
            jnp.stack(new_gv, axis=1), jnp.stack(new_dk, axis=1), jnp.stack(new_dv, axis=1))
```

```python
import functools
import math

import jax
import jax.numpy as jnp
from jax import lax
from jax.experimental import pallas as pl
from jax.experimental.pallas import tpu as pltpu

F32 = jnp.float32
BF16 = jnp.bfloat16

D_MODEL = 2048
GRID_W = 64
HEAD = 128
H_RET = 4
H_GQA = 8
KV_GQA = 2
GQA_GROUP = H_GQA // KV_GQA
H_DIFF = 4
DK_DIFF = HEAD // 2
D_FF = 11 * D_MODEL // 4
N_MOD = 9
RET_CHUNK = 128
ROPE_THETA = 10000.0
EPS = 1e-6
IN_COLS = 5120

COL_RQ, COL_RK, COL_RV, COL_RG = 0, 4, 8, 12
COL_GQ, COL_GK, COL_GV = 16, 24, 26
COL_DQ, COL_DK, COL_DV = 28, 32, 36

VMEM_LIMIT = 56 * 1024 * 1024
ROW_TILE = 1024
KEY_CHUNK = 256


def _params(sem):
    return pltpu.CompilerParams(dimension_semantics=sem, vmem_limit_bytes=VMEM_LIMIT)


def _mod_kernel(c_ref, w_ref, b_ref, o_ref):
    c = c_ref[...]
    a = (c * jax.nn.sigmoid(c)).astype(BF16)
    o_ref[...] = jnp.dot(a, w_ref[...].astype(BF16), preferred_element_type=F32) + b_ref[...]


def _modulation(cond, w, b):
    r = cond.shape[0]
    n = w.shape[1]
    tn = 1024
    return pl.pallas_call(
        _mod_kernel,
        grid=(n // tn,),
        in_specs=[pl.BlockSpec((r, D_MODEL), lambda j: (0, 0)),
                  pl.BlockSpec((D_MODEL, tn), lambda j: (0, j)),
                  pl.BlockSpec((1, tn), lambda j: (0, j))],
        out_specs=pl.BlockSpec((r, tn), lambda j: (0, j)),
        out_shape=jax.ShapeDtypeStruct((r, n), F32),
        compiler_params=_params(("arbitrary",)),
        name="modulation",
    )(cond, w, b)


def _modulated_norm(x, nw_ref, mod_ref, sub):
    ms = jnp.mean(x * x, axis=-1, keepdims=True)
    y = x * lax.rsqrt(ms + EPS) * nw_ref[sub:sub + 1, :]
    return y * (1.0 + mod_ref[0, 3 * sub + 1:3 * sub + 2, :]) + mod_ref[0, 3 * sub:3 * sub + 1, :]


def _rope(x, cos, sin_signed):
    n = x.shape[-1]
    nxt = pltpu.roll(x, n - 1, axis=1)
    prv = pltpu.roll(x, 1, axis=1)
    lane = lax.broadcasted_iota(jnp.int32, x.shape, 1)
    return x * cos + jnp.where((lane & 1) == 0, nxt, prv) * sin_signed


def _ffn_up_kernel(x_ref, mod_ref, nw_ref, wg_ref, wu_ref, o_ref, h_scr, *, sub):
    @pl.when(pl.program_id(1) == 0)
    def _():
        h_scr[...] = _modulated_norm(x_ref[...], nw_ref, mod_ref, sub).astype(BF16)

    h = h_scr[...]
    g = jnp.dot(h, wg_ref[...], preferred_element_type=F32)
    u = jnp.dot(h, wu_ref[...], preferred_element_type=F32)
    o_ref[...] = (g * jax.nn.sigmoid(g) * u).astype(BF16)


def _ffn_up(x, mod, nw, w_in, *, sub, rows_per_mod, tm, tn):
    m = x.shape[0]
    nj = D_FF // tn
    bpm = rows_per_mod // tm
    return pl.pallas_call(
        functools.partial(_ffn_up_kernel, sub=sub),
        grid=(m // tm, nj),
        in_specs=[pl.BlockSpec((tm, D_MODEL), lambda i, j: (i, 0)),
                  pl.BlockSpec((1, N_MOD, D_MODEL), lambda i, j: (i // bpm, 0, 0)),
                  pl.BlockSpec((3, D_MODEL), lambda i, j: (0, 0)),
                  pl.BlockSpec((D_MODEL, tn), lambda i, j: (0, j)),
                  pl.BlockSpec((D_MODEL, tn), lambda i, j: (0, j + nj))],
        out_specs=pl.BlockSpec((tm, tn), lambda i, j: (i, j)),
        out_shape=jax.ShapeDtypeStruct((m, D_FF), BF16),
        scratch_shapes=[pltpu.VMEM((tm, D_MODEL), BF16)],
        compiler_params=_params(("parallel", "arbitrary")),
        name="ffn_up",
    )(x, mod, nw, w_in, w_in)


def _ffn_down_kernel(h_ref, w_ref, x_ref, mod_ref, o_ref, *, gate_row):
    acc = jnp.dot(h_ref[...], w_ref[...], preferred_element_type=F32)
    o_ref[...] = x_ref[...] + (0.5 * mod_ref[0, gate_row:gate_row + 1, :]) * acc


def _ffn_down(h, w_out, x, mod, *, gate_row, rows_per_mod, tm, tn):
    m = x.shape[0]
    bpm = rows_per_mod // tm
    return pl.pallas_call(
        functools.partial(_ffn_down_kernel, gate_row=gate_row),
        grid=(m // tm, D_MODEL // tn),
        in_specs=[pl.BlockSpec((tm, D_FF), lambda i, j: (i, 0)),
                  pl.BlockSpec((D_FF, tn), lambda i, j: (0, j)),
                  pl.BlockSpec((tm, tn), lambda i, j: (i, j)),
                  pl.BlockSpec((1, N_MOD, tn), lambda i, j: (i // bpm, 0, j))],
        out_specs=pl.BlockSpec((tm, tn), lambda i, j: (i, j)),
        out_shape=jax.ShapeDtypeStruct((m, D_MODEL), F32),
        compiler_params=_params(("parallel", "arbitrary")),
        name="ffn_down",
    )(h, w_out, x, mod)


def _in_proj_kernel(x_ref, mod_ref, nw_ref, w_ref, o_ref, h_scr):
    @pl.when(pl.program_id(1) == 0)
    def _():
        h_scr[...] = _modulated_norm(x_ref[...], nw_ref, mod_ref, 1).astype(BF16)

    o_ref[...] = jnp.dot(h_scr[...], w_ref[...], preferred_element_type=F32)


def _in_proj(x, mod, nw, w_in, *, rows_per_mod, tm, tn):
    m = x.shape[0]
    bpm = rows_per_mod // tm
    return pl.pallas_call(
        _in_proj_kernel,
        grid=(m // tm, IN_COLS // tn),
        in_specs=[pl.BlockSpec((tm, D_MODEL), lambda i, j: (i, 0)),
                  pl.BlockSpec((1, N_MOD, D_MODEL), lambda i, j: (i // bpm, 0, 0)),
                  pl.BlockSpec((3, D_MODEL), lambda i, j: (0, 0)),
                  pl.BlockSpec((D_MODEL, tn), lambda i, j: (0, j))],
        out_specs=pl.BlockSpec((tm, tn), lambda i, j: (i, j)),
        out_shape=jax.ShapeDtypeStruct((m, IN_COLS), F32),
        scratch_shapes=[pltpu.VMEM((tm, D_MODEL), BF16)],
        compiler_params=_params(("parallel", "arbitrary")),
        name="mixer_in_proj",
    )(x, mod, nw, w_in)


def _out_proj_kernel(r_ref, g_ref, d_ref, w_ref, x_ref, mod_ref, o_ref):
    mixed = jnp.concatenate([r_ref[...], g_ref[...], d_ref[...]], axis=1)
    acc = jnp.dot(mixed, w_ref[...], preferred_element_type=F32)
    o_ref[...] = x_ref[...] + mod_ref[0, 5:6, :] * acc


def _out_proj(ret, gqa, dif, w_out, x, mod, *, rows_per_mod, tm):
    m = x.shape[0]
    bpm = rows_per_mod // tm
    return pl.pallas_call(
        _out_proj_kernel,
        grid=(m // tm,),
        in_specs=[pl.BlockSpec((tm, H_RET * HEAD), lambda i: (i, 0)),
                  pl.BlockSpec((tm, H_GQA * HEAD), lambda i: (i, 0)),
                  pl.BlockSpec((tm, H_DIFF * HEAD), lambda i: (i, 0)),
                  pl.BlockSpec((D_MODEL, D_MODEL), lambda i: (0, 0)),
                  pl.BlockSpec((tm, D_MODEL), lambda i: (i, 0)),
                  pl.BlockSpec((1, N_MOD, D_MODEL), lambda i: (i // bpm, 0, 0))],
        out_specs=pl.BlockSpec((tm, D_MODEL), lambda i: (i, 0)),
        out_shape=jax.ShapeDtypeStruct((m, D_MODEL), F32),
        compiler_params=_params(("parallel",)),
        name="mixer_out_proj",
    )(ret, gqa, dif, w_out, x, mod)


def _retention_kernel(*refs, seq, has_state):
    if has_state:
        q_ref, k_ref, v_ref, g_ref, dec_ref, gn_ref, st_ref, o_ref, ns_ref, of_scr = refs
    else:
        q_ref, k_ref, v_ref, g_ref, dec_ref, gn_ref, o_ref, ns_ref, of_scr = refs
        st_ref = None
    c = RET_CHUNK
    n = seq // c
    row = lax.broadcasted_iota(jnp.int32, (c, c), 0)
    col = lax.broadcasted_iota(jnp.int32, (c, c), 1)
    ridx = lax.broadcasted_iota(jnp.int32, (c, 1), 0).astype(F32)

    def log_gamma(d):
        z = dec_ref[d]
        return jnp.minimum(z, 0.0) - jnp.log(1.0 + jnp.exp(-jnp.abs(z)))

    def chunk(ci, s, lg, inner, q_dec, k_dec, c_dec):
        rows = pl.ds(pl.multiple_of(ci * c, c), c)
        qc = q_ref[rows, :]
        kc = k_ref[rows, :] * (HEAD ** -0.5)
        vc = v_ref[rows, :].astype(BF16)
        qb = qc.astype(BF16)
        att = lax.dot_general(qb, kc.astype(BF16), (((1,), (1,)), ((), ())),
                              preferred_element_type=F32) * inner
        o = (jnp.dot(att.astype(BF16), vc, preferred_element_type=F32)
             + jnp.dot(qb, s.astype(BF16), preferred_element_type=F32) * q_dec)
        kd = (kc * k_dec).astype(BF16)
        s_new = s * c_dec + lax.dot_general(kd, vc, (((0,), (0,)), ((), ())),
                                            preferred_element_type=F32)
        return rows, o, s_new

    lg = log_gamma(0)
    dif = (row - col).astype(F32)
    inner = jnp.where(row >= col, jnp.exp(jnp.maximum(dif, 0.0) * lg), 0.0)
    q_dec = jnp.exp((ridx + 1.0) * lg)
    k_dec = jnp.exp((c - 1.0 - ridx) * lg)
    c_dec = jnp.exp(float(c) * lg)
    s0 = st_ref[0] if has_state else jnp.zeros((HEAD, HEAD), F32)

    def fwd(ci, s):
        rows, o, s_new = chunk(ci, s, lg, inner, q_dec, k_dec, c_dec)
        of_scr[rows, :] = o
        return s_new

    ns_ref[0] = lax.fori_loop(0, n, fwd, s0)

    lg = log_gamma(1)
    dif = (col - row).astype(F32)
    inner = jnp.where(col >= row, jnp.exp(jnp.maximum(dif, 0.0) * lg), 0.0)
    q_dec = jnp.exp((float(c) - ridx) * lg)
    k_dec = jnp.exp(ridx * lg)
    c_dec = jnp.exp(float(c) * lg)
    s0 = st_ref[1] if has_state else jnp.zeros((HEAD, HEAD), F32)

    def bwd(t, s):
        rows, o, s_new = chunk(n - 1 - t, s, lg, inner, q_dec, k_dec, c_dec)
        o = o + of_scr[rows, :]
        xc = o - jnp.mean(o, axis=-1, keepdims=True)
        var = jnp.mean(xc * xc, axis=-1, keepdims=True)
        gate = g_ref[rows, :]
        y = (xc * lax.rsqrt(var + EPS) * gn_ref[...]) * (gate * jax.nn.sigmoid(gate))
        o_ref[rows, :] = y.astype(BF16)
        return s_new

    ns_ref[1] = lax.fori_loop(0, n, bwd, s0)


def _retention(proj, dec, gn_w, state, *, batch, seq):
    has_state = state is not None

    def col(off):
        return pl.BlockSpec((seq, HEAD), lambda b, h: (b, off + h))

    in_specs = [col(COL_RQ), col(COL_RK), col(COL_RV), col(COL_RG),
                pl.BlockSpec((2, None, 1, 1), lambda b, h: (0, h, 0, 0)),
                pl.BlockSpec((None, 1, HEAD), lambda b, h: (h, 0, 0))]
    args = [proj, proj, proj, proj, dec, gn_w]
    if has_state:
        in_specs.append(pl.BlockSpec((None, 2, None, HEAD, HEAD), lambda b, h: (b, 0, h, 0, 0)))
        args.append(state)
    out, new_state = pl.pallas_call(
        functools.partial(_retention_kernel, seq=seq, has_state=has_state),
        grid=(batch, H_RET),
        in_specs=in_specs,
        out_specs=[pl.BlockSpec((seq, HEAD), lambda b, h: (b, h)),
                   pl.BlockSpec((None, 2, None, HEAD, HEAD), lambda b, h: (b, 0, h, 0, 0))],
        out_shape=[jax.ShapeDtypeStruct((batch * seq, H_RET * HEAD), BF16),
                   jax.ShapeDtypeStruct((batch, 2, H_RET, HEAD, HEAD), F32)],
        scratch_shapes=[pltpu.VMEM((seq, HEAD), F32)],
        compiler_params=_params(("parallel", "parallel")),
        name="retention",
    )(*args)
    return out, new_state


def _softmax_pv(qq, k_scr, vt_scr, s_scr, *, n_keys, scale):
    c = scale * math.log2(math.e)
    n_chunks = n_keys // KEY_CHUNK
    m = None
    for j in range(n_chunks):
        rows = slice(j * KEY_CHUNK, (j + 1) * KEY_CHUNK)
        sj = lax.dot_general(k_scr[rows, :], qq, (((1,), (1,)), ((), ())), preferred_element_type=F32)
        s_scr[rows, :] = sj
        mj = jnp.max(sj, axis=0, keepdims=True)
        m = mj if m is None else jnp.maximum(m, mj)
    l = None
    acc = None
    for j in range(n_chunks):
        rows = slice(j * KEY_CHUNK, (j + 1) * KEY_CHUNK)
        p = jnp.exp2((s_scr[rows, :] - m) * c)
        lj = jnp.sum(p, axis=0, keepdims=True)
        aj = jnp.dot(vt_scr[:, rows], p.astype(BF16), preferred_element_type=F32)
        l = lj if l is None else l + lj
        acc = aj if acc is None else acc + aj
    return acc, l


def _gqa_kernel(*refs, seq, past, tq, latent):
    if latent:
        (q_ref, k_ref, v_ref, nw_ref, cos_ref, sin_ref, ck_ref, cv_ref,
         o_ref, k_scr, vt_scr, s_scr) = refs
    else:
        q_ref, k_ref, v_ref, nw_ref, o_ref, kn_ref, k_scr, vt_scr, s_scr = refs
    i = pl.program_id(2)

    def head_norm(x, w):
        return x * lax.rsqrt(jnp.mean(x * x, axis=-1, keepdims=True) + EPS) * w

    @pl.when(i == 0)
    def _():
        kn = head_norm(k_ref[...], nw_ref[1:2, :])
        if latent:
            k_scr[0:seq, :] = _rope(kn, cos_ref[...], sin_ref[...]).astype(BF16)
            k_scr[seq:seq + past, :] = ck_ref[...].astype(BF16)
            vt_scr[:, 0:seq] = v_ref[...].T.astype(BF16)
            vt_scr[:, seq:seq + past] = cv_ref[...].T.astype(BF16)
        else:
            kn_ref[...] = kn
            k_scr[...] = kn.astype(BF16)
            vt_scr[...] = v_ref[...].T.astype(BF16)

    q = q_ref[...]
    if latent:
        rows = pl.ds(pl.multiple_of(i * tq, tq), tq)
        cos = cos_ref[rows, :]
        sin = sin_ref[rows, :]
    heads = []
    for g in range(GQA_GROUP):
        qh = head_norm(q[:, g * HEAD:(g + 1) * HEAD], nw_ref[0:1, :])
        if latent:
            qh = _rope(qh, cos, sin)
        heads.append(qh.astype(BF16))
    qq = jnp.concatenate(heads, axis=0)
    acc, l = _softmax_pv(qq, k_scr, vt_scr, s_scr, n_keys=seq + past, scale=HEAD ** -0.5)
    o_t = acc / l
    for g in range(GQA_GROUP):
        o_ref[:, g * HEAD:(g + 1) * HEAD] = o_t[:, g * tq:(g + 1) * tq].T.astype(BF16)


def _gqa(proj, qk_norm, rope, cache_k, cache_v, layer, *, batch, seq, tq):
    latent = rope is not None
    nq = seq // tq
    past = cache_k.shape[2] if latent else 0
    in_specs = [pl.BlockSpec((tq, GQA_GROUP * HEAD), lambda b, k, i: (b * nq + i, COL_GQ // GQA_GROUP + k)),
                pl.BlockSpec((seq, HEAD), lambda b, k, i: (b, COL_GK + k)),
                pl.BlockSpec((seq, HEAD), lambda b, k, i: (b, COL_GV + k)),
                pl.BlockSpec((2, HEAD), lambda b, k, i: (0, 0))]
    args = [proj, proj, proj, qk_norm]
    out_specs = [pl.BlockSpec((tq, GQA_GROUP * HEAD), lambda b, k, i: (b * nq + i, k))]
    out_shape = [jax.ShapeDtypeStruct((batch * seq, H_GQA * HEAD), BF16)]
    if latent:
        in_specs += [pl.BlockSpec((seq, HEAD), lambda b, k, i: (0, 0)),
                     pl.BlockSpec((seq, HEAD), lambda b, k, i: (0, 0)),
                     pl.BlockSpec((None, None, past, HEAD), lambda b, k, i: (b, layer, 0, k)),
                     pl.BlockSpec((None, None, past, HEAD), lambda b, k, i: (b, layer, 0, k))]
        args += [rope[0], rope[1], cache_k, cache_v]
    else:
        out_specs.append(pl.BlockSpec((seq, HEAD), lambda b, k, i: (b, k)))
        out_shape.append(jax.ShapeDtypeStruct((batch * seq, KV_GQA * HEAD), F32))
    res = pl.pallas_call(
        functools.partial(_gqa_kernel, seq=seq, past=past, tq=tq, latent=latent),
        grid=(batch, KV_GQA, nq),
        in_specs=in_specs,
        out_specs=out_specs,
        out_shape=out_shape,
        scratch_shapes=[pltpu.VMEM((seq + past, HEAD), BF16), pltpu.VMEM((HEAD, seq + past), BF16),
                        pltpu.VMEM((seq + past, GQA_GROUP * tq), F32)],
        compiler_params=_params(("parallel", "parallel", "arbitrary")),
        name="gqa_attention",
    )(*args)
    return res if latent else (res[0], res[1])


def _diff_kernel(*refs, seq, past, tq, latent, lam_init):
    if latent:
        (q_ref, k_ref, v_ref, lam_ref, nw_ref, cos_ref, sin_ref, ck_ref, cv_ref,
         o_ref, k_scr, vt_scr, s_scr) = refs
    else:
        q_ref, k_ref, v_ref, lam_ref, nw_ref, o_ref, k_scr, vt_scr, s_scr = refs
    i = pl.program_id(2)

    @pl.when(i == 0)
    def _():
        if latent:
            k_scr[0:seq, :] = _rope(k_ref[...], cos_ref[...], sin_ref[...]).astype(BF16)
            k_scr[seq:seq + past, :] = ck_ref[...].astype(BF16)
            vt_scr[:, 0:seq] = v_ref[...].T.astype(BF16)
            vt_scr[:, seq:seq + past] = cv_ref[...].T.astype(BF16)
        else:
            k_scr[...] = k_ref[...].astype(BF16)
            vt_scr[...] = v_ref[...].T.astype(BF16)

    dl = lam_ref[...]
    lam = (jnp.exp(jnp.sum(dl[0:1, :] * dl[1:2, :], axis=-1, keepdims=True))
           - jnp.exp(jnp.sum(dl[2:3, :] * dl[3:4, :], axis=-1, keepdims=True)) + lam_init)

    q = q_ref[...]
    if latent:
        rows = pl.ds(pl.multiple_of(i * tq, tq), tq)
        q = _rope(q, cos_ref[rows, :], sin_ref[rows, :])
    lane = lax.broadcasted_iota(jnp.int32, q.shape, 1)
    q1 = jnp.where(lane < DK_DIFF, q, 0.0).astype(BF16)
    q2 = jnp.where(lane >= DK_DIFF, q, 0.0).astype(BF16)
    qq = jnp.concatenate([q1, q2], axis=0)
    acc, l = _softmax_pv(qq, k_scr, vt_scr, s_scr, n_keys=seq + past, scale=DK_DIFF ** -0.5)
    r = 1.0 / l
    o_t = acc[:, 0:tq] * r[:, 0:tq] - (lam * r[:, tq:2 * tq]) * acc[:, tq:2 * tq]
    o = o_t.T
    y = o * lax.rsqrt(jnp.mean(o * o, axis=-1, keepdims=True) + EPS) * nw_ref[...]
    o_ref[...] = (y * (1.0 - lam_init)).astype(BF16)


def _diff(proj, diff_lambda, norm_w, rope, cache_k, cache_v, layer, lam_init, *, batch, seq, tq):
    latent = rope is not None
    nq = seq // tq
    past = cache_k.shape[2] if latent else 0
    in_specs = [pl.BlockSpec((tq, HEAD), lambda b, h, i: (b * nq + i, COL_DQ + h)),
                pl.BlockSpec((seq, HEAD), lambda b, h, i: (b, COL_DK + h)),
                pl.BlockSpec((seq, HEAD), lambda b, h, i: (b, COL_DV + h)),
                pl.BlockSpec((4, DK_DIFF), lambda b, h, i: (0, 0)),
                pl.BlockSpec((None, 1, HEAD), lambda b, h, i: (h, 0, 0))]
    args = [proj, proj, proj, diff_lambda, norm_w]
    if latent:
        in_specs += [pl.BlockSpec((seq, HEAD), lambda b, h, i: (0, 0)),
                     pl.BlockSpec((seq, HEAD), lambda b, h, i: (0, 0)),
                     pl.BlockSpec((None, None, past, HEAD), lambda b, h, i: (b, layer, 0, h)),
                     pl.BlockSpec((None, None, past, HEAD), lambda b, h, i: (b, layer, 0, h))]
        args += [rope[0], rope[1], cache_k, cache_v]
    return pl.pallas_call(
        functools.partial(_diff_kernel, seq=seq, past=past, tq=tq, latent=latent, lam_init=lam_init),
        grid=(batch, H_DIFF, nq),
        in_specs=in_specs,
        out_specs=pl.BlockSpec((tq, HEAD), lambda b, h, i: (b * nq + i, h)),
        out_shape=jax.ShapeDtypeStruct((batch * seq, H_DIFF * HEAD), BF16),
        scratch_shapes=[pltpu.VMEM((seq + past, HEAD), BF16), pltpu.VMEM((HEAD, seq + past), BF16),
                        pltpu.VMEM((seq + past, 2 * tq), F32)],
        compiler_params=_params(("parallel", "parallel", "arbitrary")),
        name="diff_attention",
    )(*args)


def _final_norm_kernel(x_ref, w_ref, o_ref):
    x = x_ref[...]
    o_ref[...] = x * lax.rsqrt(jnp.mean(x * x, axis=-1, keepdims=True) + EPS) * w_ref[...]


def _final_norm(x, w, *, tm):
    m = x.shape[0]
    return pl.pallas_call(
        _final_norm_kernel,
        grid=(m // tm,),
        in_specs=[pl.BlockSpec((tm, D_MODEL), lambda i: (i, 0)),
                  pl.BlockSpec((1, D_MODEL), lambda i: (0, 0))],
        out_specs=pl.BlockSpec((tm, D_MODEL), lambda i: (i, 0)),
        out_shape=jax.ShapeDtypeStruct((m, D_MODEL), F32),
        compiler_params=_params(("parallel",)),
        name="final_norm",
    )(x, w)


def _rope_tables(rows, d, width):
    axis = d // 2
    inv = 1.0 / (ROPE_THETA ** (jnp.arange(0, axis, 2, dtype=F32) / axis))
    r = jnp.repeat(jnp.arange(rows, dtype=F32), GRID_W)
    cl = jnp.tile(jnp.arange(GRID_W, dtype=F32), rows)
    ang = jnp.concatenate([r[:, None] * inv, cl[:, None] * inv], axis=-1)
    cos = jnp.repeat(jnp.cos(ang), 2, axis=-1)
    sin = jnp.repeat(jnp.sin(ang), 2, axis=-1) * jnp.tile(jnp.asarray([-1.0, 1.0], F32), d // 2)
    reps = width // d
    return jnp.tile(cos, (1, reps)), jnp.tile(sin, (1, reps))


def _trunk_layer(x, mod, lw, *, batch, seq, tm, tq, lam_init, layer, ctx):
    rows_per_mod = seq if mod.shape[0] > 1 else batch * seq
    kw = dict(rows_per_mod=rows_per_mod, tm=tm)
    h = _ffn_up(x, mod, lw['norm_w'], lw['ffn_w_in'][0], sub=0, tn=512, **kw)
    x = _ffn_down(h, lw['ffn_w_out'][0], x, mod, gate_row=2, tn=512, **kw)

    proj = _in_proj(x, mod, lw['norm_w'], lw['w_in'], tn=1024, **kw)
    if ctx is None:
        ret, new_state = _retention(proj, lw['ret_decay'], lw['ret_gn_w'], None, batch=batch, seq=seq)
        gqa, k_norm = _gqa(proj, lw['gqa_qk_norm'], None, None, None, layer, batch=batch, seq=seq, tq=tq)
        dif = _diff(proj, lw['diff_lambda'], lw['diff_norm_w'], None, None, None, layer, lam_init,
                    batch=batch, seq=seq, tq=2 * tq)
        new = (new_state,
               k_norm.reshape(batch, seq, KV_GQA, HEAD),
               proj[:, COL_GV * HEAD:(COL_GV + KV_GQA) * HEAD].reshape(batch, seq, KV_GQA, HEAD),
               proj[:, COL_DK * HEAD:(COL_DK + H_DIFF) * HEAD].reshape(batch, seq, H_DIFF, HEAD),
               proj[:, COL_DV * HEAD:(COL_DV + H_DIFF) * HEAD].reshape(batch, seq, H_DIFF, HEAD))
    else:
        ret, _ = _retention(proj, lw['ret_decay'], lw['ret_gn_w'], ctx['state_ret'], batch=batch, seq=seq)
        gqa = _gqa(proj, lw['gqa_qk_norm'], ctx['rope_g'], ctx['k_gqa'], ctx['v_gqa'], layer,
                   batch=batch, seq=seq, tq=tq)[0]
        dif = _diff(proj, lw['diff_lambda'], lw['diff_norm_w'], ctx['rope_d'], ctx['k_diff'], ctx['v_diff'],
                    layer, lam_init, batch=batch, seq=seq, tq=2 * tq)
        new = None
    x = _out_proj(ret, gqa, dif, lw['w_out'], x, mod, rows_per_mod=rows_per_mod, tm=min(512, tm))

    h = _ffn_up(x, mod, lw['norm_w'], lw['ffn_w_in'][1], sub=2, tn=512, **kw)
    x = _ffn_down(h, lw['ffn_w_out'][1], x, mod, gate_row=8, tn=512, **kw)
    return x, new


def kernel(x_prompt, x_sample, c, state_ret, cache_gqa_k, cache_gqa_v, cache_diff_k, cache_diff_v, c_ctx,
           w_mod, b_mod, norm_w, ffn_w_in, ffn_w_out, w_in, w_out, ret_decay, ret_gn_w, gqa_qk_norm,
           diff_lambda, diff_norm_w, final_norm_w):
    bp, sp, _ = x_prompt.shape
    bs, ss, _ = x_sample.shape
    depth = w_in.shape[0]
    past = cache_gqa_k.shape[2]
    rows = ss // GRID_W
    rope_g = _rope_tables(rows, HEAD, HEAD)
    rope_d = _rope_tables(rows, DK_DIFF, HEAD)

    n_cond = 16
    cond = jnp.zeros((n_cond, D_MODEL), F32).at[0].set(c_ctx).at[1:1 + bs].set(c)

    ck_g = cache_gqa_k.reshape(bs, depth, past, KV_GQA * HEAD)
    cv_g = cache_gqa_v.reshape(bs, depth, past, KV_GQA * HEAD)
    ck_d = cache_diff_k.reshape(bs, depth, past, H_DIFF * HEAD)
    cv_d = cache_diff_v.reshape(bs, depth, past, H_DIFF * HEAD)

    xp = x_prompt.reshape(bp * sp, D_MODEL)
    xs = x_sample.reshape(bs * ss, D_MODEL)
    new_ret, new_gk, new_gv, new_dk, new_dv = [], [], [], [], []
    for l in range(depth):
        lam_init = 0.8 - 0.6 * math.exp(-0.3 * l)
        m = _modulation(cond, w_mod[l], b_mod[l].reshape(1, -1))
        mod_p = m[0:1].reshape(1, N_MOD, D_MODEL)
        mod_s = m[1:1 + bs].reshape(bs, N_MOD, D_MODEL)
        lw = {'norm_w': norm_w[l],
              'ffn_w_in': ffn_w_in[l].astype(BF16), 'ffn_w_out': ffn_w_out[l].astype(BF16),
              'w_in': w_in[l].astype(BF16), 'w_out': w_out[l].astype(BF16),
              'ret_decay': ret_decay[l].reshape(2, H_RET, 1, 1),
              'ret_gn_w': ret_gn_w[l].reshape(H_RET, 1, HEAD),
              'gqa_qk_norm': gqa_qk_norm[l],
              'diff_lambda': diff_lambda[l],
              'diff_norm_w': diff_norm_w[l].reshape(H_DIFF, 1, HEAD)}
        xp, new = _trunk_layer(xp, mod_p, lw, batch=bp, seq=sp, tm=min(ROW_TILE, bp * sp), tq=128,
                               lam_init=lam_init, layer=l, ctx=None)
        new_ret.append(new[0])
        new_gk.append(new[1])
        new_gv.append(new[2])
        new_dk.append(new[3])
        new_dv.append(new[4])
        ctx = {'state_ret': state_ret[:, l], 'k_gqa': ck_g, 'v_gqa': cv_g, 'k_diff': ck_d, 'v_diff': cv_d,
               'rope_g': rope_g, 'rope_d': rope_d}
        xs, _ = _trunk_layer(xs, mod_s, lw, batch=bs, seq=ss, tm=min(ROW_TILE, ss), tq=128,
                             lam_init=lam_init, layer=l, ctx=ctx)
    fw = final_norm_w.reshape(1, D_MODEL)
    y_prompt = _final_norm(xp, fw, tm=min(ROW_TILE, bp * sp)).reshape(bp, sp, D_MODEL)
    y_sample = _final_norm(xs, fw, tm=min(ROW_TILE, ss)).reshape(bs, ss, D_MODEL)
    return (y_prompt, y_sample, jnp.stack(new_ret, axis=1), jnp.stack(new_gk, axis=1),
            jnp.stack(new_gv, axis=1), jnp.stack(new_dk, axis=1), jnp.stack(new_dv, axis=1))
```

```python
import functools
import math

import jax
import jax.numpy as jnp
from jax import lax
from jax.experimental import pallas as pl
from jax.experimental.pallas import tpu as pltpu

F32 = jnp.float32
BF16 = jnp.bfloat16

D_MODEL = 2048
GRID_W = 64
HEAD = 128
H_RET = 4
H_GQA = 8
KV_GQA = 2
GQA_GROUP = H_GQA // KV_GQA
H_DIFF = 4
DK_DIFF = HEAD // 2
D_FF = 11 * D_MODEL // 4
N_MOD = 9
RET_CHUNK = 128
ROPE_THETA = 10000.0
EPS = 1e-6
IN_COLS = 5120

COL_RQ, COL_RK, COL_RV, COL_RG = 0, 4, 8, 12
COL_GQ, COL_GK, COL_GV = 16, 24, 26
COL_DQ, COL_DK, COL_DV = 28, 32, 36

VMEM_LIMIT = 56 * 1024 * 1024
ROW_TILE = 1024
KEY_CHUNK = 256
N_COND = 16

EXP2_SCALE_GQA = HEAD ** -0.5 * math.log2(math.e)
EXP2_SCALE_DIFF = DK_DIFF ** -0.5 * math.log2(math.e)

_NT = (((1,), (1,)), ((), ()))


def _params(sem):
    return pltpu.CompilerParams(dimension_semantics=sem, vmem_limit_bytes=VMEM_LIMIT)


def _mod_kernel(c_ref, w_ref, b_ref, o_ref):
    c = c_ref[...]
    a = (c * jax.nn.sigmoid(c)).astype(BF16)
    o_ref[...] = jnp.dot(a, w_ref[...].astype(BF16), preferred_element_type=F32) + b_ref[...]


def _modulation(cond, w_mod, b_mod, layer):
    r = cond.shape[0]
    n = w_mod.shape[2]
    tn = 1024
    return pl.pallas_call(
        _mod_kernel,
        grid=(n // tn,),
        in_specs=[pl.BlockSpec((r, D_MODEL), lambda j: (0, 0)),
                  pl.BlockSpec((None, D_MODEL, tn), lambda j: (layer, 0, j)),
                  pl.BlockSpec((None, 1, tn), lambda j: (layer, 0, j))],
        out_specs=pl.BlockSpec((r, tn), lambda j: (0, j)),
        out_shape=jax.ShapeDtypeStruct((r, n), F32),
        compiler_params=_params(("arbitrary",)),
        name="modulation",
    )(cond, w_mod, b_mod)


def _modulated_norm(x, nw_ref, mod_ref, sub):
    ms = jnp.mean(x * x, axis=-1, keepdims=True)
    y = x * lax.rsqrt(ms + EPS) * nw_ref[sub:sub + 1, :]
    return y * (1.0 + mod_ref[0, 3 * sub + 1:3 * sub + 2, :]) + mod_ref[0, 3 * sub:3 * sub + 1, :]


def _head_norm(x, w):
    return x * lax.rsqrt(jnp.mean(x * x, axis=-1, keepdims=True) + EPS) * w


def _rope(x, cos, sin_signed):
    n = x.shape[-1]
    nxt = pltpu.roll(x, n - 1, axis=1)
    prv = pltpu.roll(x, 1, axis=1)
    lane = lax.broadcasted_iota(jnp.int32, x.shape, 1)
    return x * cos + jnp.where((lane & 1) == 0, nxt, prv) * sin_signed


def _mod_spec(tm, width, rows_per_mod, mod_row0, col_axis):
    bpm = rows_per_mod // tm
    if col_axis:
        return pl.BlockSpec((1, N_MOD, width), lambda i, j: (mod_row0 + i // bpm, 0, j))
    return pl.BlockSpec((1, N_MOD, width), lambda i, *_: (mod_row0 + i // bpm, 0, 0))


def _ffn_up_kernel(x_ref, mod_ref, nw_ref, wg_ref, wu_ref, o_ref, h_scr, *, sub):
    @pl.when(pl.program_id(1) == 0)
    def _():
        h_scr[...] = _modulated_norm(x_ref[...], nw_ref, mod_ref, sub).astype(BF16)

    h = h_scr[...]
    g = jnp.dot(h, wg_ref[...], preferred_element_type=F32)
    u = jnp.dot(h, wu_ref[...], preferred_element_type=F32)
    o_ref[...] = (g * jax.nn.sigmoid(g) * u).astype(BF16)


def _ffn_up(x, mod, w, *, layer, which, sub, rows_per_mod, mod_row0, tm, tn):
    m = x.shape[0]
    nj = D_FF // tn
    return pl.pallas_call(
        functools.partial(_ffn_up_kernel, sub=sub),
        grid=(m // tm, nj),
        in_specs=[pl.BlockSpec((tm, D_MODEL), lambda i, j: (i, 0)),
                  _mod_spec(tm, D_MODEL, rows_per_mod, mod_row0, False),
                  pl.BlockSpec((None, 3, D_MODEL), lambda i, j: (layer, 0, 0)),
                  pl.BlockSpec((None, None, D_MODEL, tn), lambda i, j: (layer, which, 0, j)),
                  pl.BlockSpec((None, None, D_MODEL, tn), lambda i, j: (layer, which, 0, j + nj))],
        out_specs=pl.BlockSpec((tm, tn), lambda i, j: (i, j)),
        out_shape=jax.ShapeDtypeStruct((m, D_FF), BF16),
        scratch_shapes=[pltpu.VMEM((tm, D_MODEL), BF16)],
        compiler_params=_params(("parallel", "arbitrary")),
        name="ffn_up",
    )(x, mod, w['norm_w'], w['ffn_w_in'], w['ffn_w_in'])


def _ffn_down_kernel(h_ref, w_ref, x_ref, mod_ref, o_ref, *, gate_row):
    acc = jnp.dot(h_ref[...], w_ref[...], preferred_element_type=F32)
    o_ref[...] = x_ref[...] + (0.5 * mod_ref[0, gate_row:gate_row + 1, :]) * acc


def _ffn_down(h, x, mod, w, *, layer, which, gate_row, rows_per_mod, mod_row0, tm, tn):
    m = x.shape[0]
    return pl.pallas_call(
        functools.partial(_ffn_down_kernel, gate_row=gate_row),
        grid=(m // tm, D_MODEL // tn),
        in_specs=[pl.BlockSpec((tm, D_FF), lambda i, j: (i, 0)),
                  pl.BlockSpec((None, None, D_FF, tn), lambda i, j: (layer, which, 0, j)),
                  pl.BlockSpec((tm, tn), lambda i, j: (i, j)),
                  _mod_spec(tm, tn, rows_per_mod, mod_row0, True)],
        out_specs=pl.BlockSpec((tm, tn), lambda i, j: (i, j)),
        out_shape=jax.ShapeDtypeStruct((m, D_MODEL), F32),
        compiler_params=_params(("parallel", "arbitrary")),
        name="ffn_down",
    )(h, w['ffn_w_out'], x, mod)


def _in_proj_kernel(x_ref, mod_ref, nw_ref, w_ref, o_ref, h_scr):
    @pl.when(pl.program_id(1) == 0)
    def _():
        h_scr[...] = _modulated_norm(x_ref[...], nw_ref, mod_ref, 1).astype(BF16)

    o_ref[...] = jnp.dot(h_scr[...], w_ref[...], preferred_element_type=F32)


def _in_proj(x, mod, w, *, layer, rows_per_mod, mod_row0, tm, tn):
    m = x.shape[0]
    return pl.pallas_call(
        _in_proj_kernel,
        grid=(m // tm, IN_COLS // tn),
        in_specs=[pl.BlockSpec((tm, D_MODEL), lambda i, j: (i, 0)),
                  _mod_spec(tm, D_MODEL, rows_per_mod, mod_row0, False),
                  pl.BlockSpec((None, 3, D_MODEL), lambda i, j: (layer, 0, 0)),
                  pl.BlockSpec((None, D_MODEL, tn), lambda i, j: (layer, 0, j))],
        out_specs=pl.BlockSpec((tm, tn), lambda i, j: (i, j)),
        out_shape=jax.ShapeDtypeStruct((m, IN_COLS), F32),
        scratch_shapes=[pltpu.VMEM((tm, D_MODEL), BF16)],
        compiler_params=_params(("parallel", "arbitrary")),
        name="mixer_in_proj",
    )(x, mod, w['norm_w'], w['w_in'])


def _latent_in_proj_kernel(x_ref, mod_ref, nw_ref, w_ref, qkn_ref, cg_ref, sg_ref, cd_ref, sd_ref,
                           ret_ref, gq_ref, gk_ref, gvt_ref, dq_ref, dk_ref, dvt_ref):
    h = _modulated_norm(x_ref[...], nw_ref, mod_ref, 1).astype(BF16)

    def proj(slot0, n_slots=4):
        return jnp.dot(h, w_ref[:, slot0 * HEAD:(slot0 + n_slots) * HEAD], preferred_element_type=F32)

    def slot(y, s):
        return y[:, s * HEAD:(s + 1) * HEAD]

    for g in range(4):
        ret_ref[:, 4 * g * HEAD:4 * (g + 1) * HEAD] = proj(4 * g)

    cg, sg = cg_ref[...], sg_ref[...]
    wq, wk = qkn_ref[0:1, :], qkn_ref[1:2, :]
    for half in range(2):
        y = proj(COL_GQ + 4 * half)
        for s in range(4):
            q = _rope(_head_norm(slot(y, s), wq), cg, sg) * EXP2_SCALE_GQA
            gq_ref[:, (4 * half + s) * HEAD:(4 * half + s + 1) * HEAD] = q.astype(BF16)
    y = proj(COL_GK)
    for s in range(KV_GQA):
        gk_ref[:, s * HEAD:(s + 1) * HEAD] = _rope(_head_norm(slot(y, s), wk), cg, sg).astype(BF16)
    gvt_ref[...] = y[:, KV_GQA * HEAD:2 * KV_GQA * HEAD].T.astype(BF16)

    cd, sd = cd_ref[...], sd_ref[...]
    y = proj(COL_DQ)
    for s in range(H_DIFF):
        dq_ref[:, s * HEAD:(s + 1) * HEAD] = (_rope(slot(y, s), cd, sd) * EXP2_SCALE_DIFF).astype(BF16)
    y = proj(COL_DK)
    for s in range(H_DIFF):
        dk_ref[:, s * HEAD:(s + 1) * HEAD] = _rope(slot(y, s), cd, sd).astype(BF16)
    dvt_ref[...] = proj(COL_DV).T.astype(BF16)


def _latent_in_proj(x, mod, w, rope_g, rope_d, *, layer, seq, mod_row0, tm):
    m = x.shape[0]
    tiles_per_seq = seq // tm

    def rows(width, dtype):
        return pl.BlockSpec((tm, width), lambda i: (i, 0)), jax.ShapeDtypeStruct((m, width), dtype)

    def cols(height):
        return pl.BlockSpec((height, tm), lambda i: (0, i)), jax.ShapeDtypeStruct((height, m), BF16)

    def table():
        return pl.BlockSpec((tm, HEAD), lambda i: (i % tiles_per_seq, 0))

    outs = [rows(16 * HEAD, F32), rows(H_GQA * HEAD, BF16), rows(KV_GQA * HEAD, BF16), cols(KV_GQA * HEAD),
            rows(H_DIFF * HEAD, BF16), rows(H_DIFF * HEAD, BF16), cols(H_DIFF * HEAD)]
    return pl.pallas_call(
        _latent_in_proj_kernel,
        grid=(m // tm,),
        in_specs=[pl.BlockSpec((tm, D_MODEL), lambda i: (i, 0)),
                  _mod_spec(tm, D_MODEL, seq, mod_row0, False),
                  pl.BlockSpec((None, 3, D_MODEL), lambda i: (layer, 0, 0)),
                  pl.BlockSpec((None, D_MODEL, IN_COLS), lambda i: (layer, 0, 0),
                               pipeline_mode=pl.Buffered(1)),
                  pl.BlockSpec((None, 2, HEAD), lambda i: (layer, 0, 0)),
                  table(), table(), table(), table()],
        out_specs=[o[0] for o in outs],
        out_shape=[o[1] for o in outs],
        compiler_params=_params(("parallel",)),
        name="latent_in_proj",
    )(x, mod, w['norm_w'], w['w_in'], w['gqa_qk_norm'], rope_g[0], rope_g[1], rope_d[0], rope_d[1])


def _out_proj_kernel(r_ref, g_ref, d_ref, w_ref, x_ref, mod_ref, o_ref):
    mixed = jnp.concatenate([r_ref[...], g_ref[...], d_ref[...]], axis=1)
    acc = jnp.dot(mixed, w_ref[...], preferred_element_type=F32)
    o_ref[...] = x_ref[...] + mod_ref[0, 5:6, :] * acc


def _out_proj(ret, gqa, dif, x, mod, w, *, layer, rows_per_mod, mod_row0, tm):
    m = x.shape[0]
    return pl.pallas_call(
        _out_proj_kernel,
        grid=(m // tm,),
        in_specs=[pl.BlockSpec((tm, H_RET * HEAD), lambda i: (i, 0)),
                  pl.BlockSpec((tm, H_GQA * HEAD), lambda i: (i, 0)),
                  pl.BlockSpec((tm, H_DIFF * HEAD), lambda i: (i, 0)),
                  pl.BlockSpec((None, D_MODEL, D_MODEL), lambda i: (layer, 0, 0)),
                  pl.BlockSpec((tm, D_MODEL), lambda i: (i, 0)),
                  _mod_spec(tm, D_MODEL, rows_per_mod, mod_row0, False)],
        out_specs=pl.BlockSpec((tm, D_MODEL), lambda i: (i, 0)),
        out_shape=jax.ShapeDtypeStruct((m, D_MODEL), F32),
        compiler_params=_params(("parallel",)),
        name="mixer_out_proj",
    )(ret, gqa, dif, w['w_out'], x, mod)


def _retention_kernel(*refs, seq, has_state):
    if has_state:
        q_ref, k_ref, v_ref, g_ref, dec_ref, gn_ref, st_ref, o_ref, ns_ref, of_scr, ob_scr = refs
    else:
        q_ref, k_ref, v_ref, g_ref, dec_ref, gn_ref, o_ref, ns_ref, of_scr, ob_scr = refs
        st_ref = None
    c = RET_CHUNK
    n = seq // c
    row = lax.broadcasted_iota(jnp.int32, (c, c), 0)
    col = lax.broadcasted_iota(jnp.int32, (c, c), 1)
    ridx = lax.broadcasted_iota(jnp.int32, (c, 1), 0).astype(F32)

    def log_gamma(d):
        z = dec_ref[d]
        return jnp.minimum(z, 0.0) - jnp.log(1.0 + jnp.exp(-jnp.abs(z)))

    def chunk(ci, s, consts):
        inner, q_dec, k_dec, c_dec = consts
        rows = pl.ds(pl.multiple_of(ci * c, c), c)
        kc = k_ref[rows, :] * (HEAD ** -0.5)
        vc = v_ref[rows, :].astype(BF16)
        qb = q_ref[rows, :].astype(BF16)
        att = lax.dot_general(qb, kc.astype(BF16), _NT, preferred_element_type=F32) * inner
        o = (jnp.dot(att.astype(BF16), vc, preferred_element_type=F32)
             + jnp.dot(qb, s.astype(BF16), preferred_element_type=F32) * q_dec)
        kd = (kc * k_dec).astype(BF16)
        s_new = s * c_dec + lax.dot_general(kd, vc, (((0,), (0,)), ((), ())),
                                            preferred_element_type=F32)
        return rows, o, s_new

    lg = log_gamma(0)
    fwd = (jnp.where(row >= col, jnp.exp(jnp.maximum((row - col).astype(F32), 0.0) * lg), 0.0),
           jnp.exp((ridx + 1.0) * lg), jnp.exp((c - 1.0 - ridx) * lg), jnp.exp(float(c) * lg))
    lg = log_gamma(1)
    bwd = (jnp.where(col >= row, jnp.exp(jnp.maximum((col - row).astype(F32), 0.0) * lg), 0.0),
           jnp.exp((float(c) - ridx) * lg), jnp.exp(ridx * lg), jnp.exp(float(c) * lg))
    zero = jnp.zeros((HEAD, HEAD), F32)
    s0 = (st_ref[0], st_ref[1]) if has_state else (zero, zero)

    def both(t, carry):
        rows, o, s_f = chunk(t, carry[0], fwd)
        of_scr[rows, :] = o
        rows, o, s_b = chunk(n - 1 - t, carry[1], bwd)
        ob_scr[rows, :] = o
        return s_f, s_b

    s_f, s_b = lax.fori_loop(0, n, both, s0, unroll=2)
    ns_ref[0] = s_f
    ns_ref[1] = s_b

    def finish(ci, carry):
        rows = pl.ds(pl.multiple_of(ci * c, c), c)
        o = of_scr[rows, :] + ob_scr[rows, :]
        xc = o - jnp.mean(o, axis=-1, keepdims=True)
        var = jnp.mean(xc * xc, axis=-1, keepdims=True)
        gate = g_ref[rows, :]
        y = (xc * lax.rsqrt(var + EPS) * gn_ref[...]) * (gate * jax.nn.sigmoid(gate))
        o_ref[rows, :] = y.astype(BF16)
        return carry

    lax.fori_loop(0, n, finish, 0, unroll=2)


def _retention(proj, w, state, *, layer, batch, seq):
    has_state = state is not None

    def col(off):
        return pl.BlockSpec((seq, HEAD), lambda b, h: (b, off + h))

    in_specs = [col(COL_RQ), col(COL_RK), col(COL_RV), col(COL_RG),
                pl.BlockSpec((None, 2, None, 1, 1), lambda b, h: (layer, 0, h, 0, 0)),
                pl.BlockSpec((None, None, 1, HEAD), lambda b, h: (layer, h, 0, 0))]
    args = [proj, proj, proj, proj, w['ret_decay'], w['ret_gn_w']]
    if has_state:
        in_specs.append(pl.BlockSpec((None, None, 2, None, HEAD, HEAD), lambda b, h: (b, layer, 0, h, 0, 0)))
        args.append(state)
    out, new_state = pl.pallas_call(
        functools.partial(_retention_kernel, seq=seq, has_state=has_state),
        grid=(batch, H_RET),
        in_specs=in_specs,
        out_specs=[pl.BlockSpec((seq, HEAD), lambda b, h: (b, h)),
                   pl.BlockSpec((None, 2, None, HEAD, HEAD), lambda b, h: (b, 0, h, 0, 0))],
        out_shape=[jax.ShapeDtypeStruct((batch * seq, H_RET * HEAD), BF16),
                   jax.ShapeDtypeStruct((batch, 2, H_RET, HEAD, HEAD), F32)],
        scratch_shapes=[pltpu.VMEM((seq, HEAD), F32), pltpu.VMEM((seq, HEAD), F32)],
        compiler_params=_params(("parallel", "parallel")),
        name="retention",
    )(*args)
    return out, new_state


def _softmax_pv(qq, k_scr, vt_scr, s_scr, *, n_keys, scale):
    c = scale * math.log2(math.e)
    n_chunks = n_keys // KEY_CHUNK
    m = None
    for j in range(n_chunks):
        rows = slice(j * KEY_CHUNK, (j + 1) * KEY_CHUNK)
        sj = lax.dot_general(k_scr[rows, :], qq, _NT, preferred_element_type=F32)
        s_scr[rows, :] = sj
        mj = jnp.max(sj, axis=0, keepdims=True)
        m = mj if m is None else jnp.maximum(m, mj)
    l = None
    acc = None
    for j in range(n_chunks):
        rows = slice(j * KEY_CHUNK, (j + 1) * KEY_CHUNK)
        p = jnp.exp2((s_scr[rows, :] - m) * c)
        lj = jnp.sum(p, axis=0, keepdims=True)
        aj = jnp.dot(vt_scr[:, rows], p.astype(BF16), preferred_element_type=F32)
        l = lj if l is None else l + lj
        acc = aj if acc is None else acc + aj
    return acc, l


def _diff_lambda(lam_ref, lam_init):
    dl = lam_ref[...]
    return (jnp.exp(jnp.sum(dl[0:1, :] * dl[1:2, :], axis=-1, keepdims=True))
            - jnp.exp(jnp.sum(dl[2:3, :] * dl[3:4, :], axis=-1, keepdims=True)) + lam_init)


def _diff_combine(acc, l, lam, nw, tq, lam_init):
    r = 1.0 / l
    o = (acc[:, 0:tq] * r[:, 0:tq] - (lam * r[:, tq:2 * tq]) * acc[:, tq:2 * tq]).T
    return (_head_norm(o, nw) * (1.0 - lam_init)).astype(BF16)


def _split_halves(q):
    lane = lax.broadcasted_iota(jnp.int32, q.shape, 1)
    return jnp.concatenate([jnp.where(lane < DK_DIFF, q, 0.0), jnp.where(lane >= DK_DIFF, q, 0.0)],
                           axis=0).astype(BF16)


def _ctx_gqa_kernel(q_ref, k_ref, v_ref, nw_ref, o_ref, kn_ref, k_scr, vt_scr, s_scr, *, seq, tq):
    @pl.when(pl.program_id(2) == 0)
    def _():
        kn = _head_norm(k_ref[...], nw_ref[1:2, :])
        kn_ref[...] = kn
        k_scr[...] = kn.astype(BF16)
        vt_scr[...] = v_ref[...].T.astype(BF16)

    q = q_ref[...]
    qq = jnp.concatenate([_head_norm(q[:, g * HEAD:(g + 1) * HEAD], nw_ref[0:1, :]).astype(BF16)
                          for g in range(GQA_GROUP)], axis=0)
    acc, l = _softmax_pv(qq, k_scr, vt_scr, s_scr, n_keys=seq, scale=HEAD ** -0.5)
    o_t = acc / l
    for g in range(GQA_GROUP):
        o_ref[:, g * HEAD:(g + 1) * HEAD] = o_t[:, g * tq:(g + 1) * tq].T.astype(BF16)


def _ctx_gqa(proj, w, *, layer, batch, seq, tq):
    nq = seq // tq
    return pl.pallas_call(
        functools.partial(_ctx_gqa_kernel, seq=seq, tq=tq),
        grid=(batch, KV_GQA, nq),
        in_specs=[pl.BlockSpec((tq, GQA_GROUP * HEAD), lambda b, k, i: (b * nq + i, COL_GQ // GQA_GROUP + k)),
                  pl.BlockSpec((seq, HEAD), lambda b, k, i: (b, COL_GK + k)),
                  pl.BlockSpec((seq, HEAD), lambda b, k, i: (b, COL_GV + k)),
                  pl.BlockSpec((None, 2, HEAD), lambda b, k, i: (layer, 0, 0))],
        out_specs=[pl.BlockSpec((tq, GQA_GROUP * HEAD), lambda b, k, i: (b * nq + i, k)),
                   pl.BlockSpec((seq, HEAD), lambda b, k, i: (b, k))],
        out_shape=[jax.ShapeDtypeStruct((batch * seq, H_GQA * HEAD), BF16),
                   jax.ShapeDtypeStruct((batch * seq, KV_GQA * HEAD), F32)],
        scratch_shapes=[pltpu.VMEM((seq, HEAD), BF16), pltpu.VMEM((HEAD, seq), BF16),
                        pltpu.VMEM((seq, GQA_GROUP * tq), F32)],
        compiler_params=_params(("parallel", "parallel", "arbitrary")),
        name="ctx_gqa_attention",
    )(proj, proj, proj, w['gqa_qk_norm'])


def _ctx_diff_kernel(q_ref, k_ref, v_ref, lam_ref, nw_ref, o_ref, k_scr, vt_scr, s_scr, *, seq, tq, lam_init):
    @pl.when(pl.program_id(2) == 0)
    def _():
        k_scr[...] = k_ref[...].astype(BF16)
        vt_scr[...] = v_ref[...].T.astype(BF16)

    lam = _diff_lambda(lam_ref, lam_init)
    acc, l = _softmax_pv(_split_halves(q_ref[...]), k_scr, vt_scr, s_scr, n_keys=seq, scale=DK_DIFF ** -0.5)
    o_ref[...] = _diff_combine(acc, l, lam, nw_ref[...], tq, lam_init)


def _ctx_diff(proj, w, *, layer, lam_init, batch, seq, tq):
    nq = seq // tq
    return pl.pallas_call(
        functools.partial(_ctx_diff_kernel, seq=seq, tq=tq, lam_init=lam_init),
        grid=(batch, H_DIFF, nq),
        in_specs=[pl.BlockSpec((tq, HEAD), lambda b, h, i: (b * nq + i, COL_DQ + h)),
                  pl.BlockSpec((seq, HEAD), lambda b, h, i: (b, COL_DK + h)),
                  pl.BlockSpec((seq, HEAD), lambda b, h, i: (b, COL_DV + h)),
                  pl.BlockSpec((None, 4, DK_DIFF), lambda b, h, i: (layer, 0, 0)),
                  pl.BlockSpec((None, None, 1, HEAD), lambda b, h, i: (layer, h, 0, 0))],
        out_specs=pl.BlockSpec((tq, HEAD), lambda b, h, i: (b * nq + i, h)),
        out_shape=jax.ShapeDtypeStruct((batch * seq, H_DIFF * HEAD), BF16),
        scratch_shapes=[pltpu.VMEM((seq, HEAD), BF16), pltpu.VMEM((HEAD, seq), BF16),
                        pltpu.VMEM((seq, 2 * tq), F32)],
        compiler_params=_params(("parallel", "parallel", "arbitrary")),
        name="ctx_diff_attention",
    )(proj, proj, proj, w['diff_lambda'], w['diff_norm_w'])


def _latent_attn_kernel(*refs, seq, past, tq, mode, lam_init):
    if mode == 'gqa':
        q_ref, k_ref, vt_ref, ck_ref, cv_ref, o_ref, ck_scr, cvt_scr, s0_scr, s1_scr = refs
        group = GQA_GROUP
    else:
        q_ref, k_ref, vt_ref, ck_ref, cv_ref, lam_ref, nw_ref, o_ref, ck_scr, cvt_scr, s0_scr, s1_scr = refs
        group = 2
        lam = _diff_lambda(lam_ref, lam_init)
        nw = nw_ref[...]
    n_tiles = seq // tq
    n_live = seq // KEY_CHUNK
    n_chunks = (seq + past) // KEY_CHUNK
    ck_scr[...] = ck_ref[...].astype(BF16)
    cvt_scr[...] = cv_ref[...].T.astype(BF16)

    def chunk_rows(j):
        return slice(j * KEY_CHUNK, (j + 1) * KEY_CHUNK)

    def k_chunk(j):
        return k_ref[chunk_rows(j), :] if j < n_live else ck_scr[chunk_rows(j - n_live), :]

    def vt_chunk(j):
        return vt_ref[:, chunk_rows(j)] if j < n_live else cvt_scr[:, chunk_rows(j - n_live)]

    def tile_rows(t):
        return pl.ds(pl.multiple_of(t * tq, tq), tq)

    def stacked_q(t):
        rows = tile_rows(t)
        if mode == 'gqa':
            return jnp.concatenate([q_ref[rows, g * HEAD:(g + 1) * HEAD] for g in range(group)], axis=0)
        return _split_halves(q_ref[rows, :].astype(F32))

    def scores(qq, j, s_scr):
        sj = lax.dot_general(k_chunk(j), qq, _NT, preferred_element_type=F32)
        s_scr[chunk_rows(j), :] = sj
        return jnp.max(sj, axis=0, keepdims=True)

    def weights(j, s_scr, m):
        p = jnp.exp2(s_scr[chunk_rows(j), :] - m)
        return (jnp.sum(p, axis=0, keepdims=True),
                jnp.dot(vt_chunk(j), p.astype(BF16), preferred_element_type=F32))

    def finish(t, acc, l):
        rows = tile_rows(t)
        if mode == 'gqa':
            o_t = acc / l
            for g in range(group):
                o_ref[rows, g * HEAD:(g + 1) * HEAD] = o_t[:, g * tq:(g + 1) * tq].T.astype(BF16)
        else:
            o_ref[rows, :] = _diff_combine(acc, l, lam, nw, tq, lam_init)

    def merge(a, b, op):
        return b if a is None else op(a, b)

    def step(t_cur, t_nxt, s_cur, s_nxt, m_cur):
        qq = stacked_q(t_nxt)
        m_nxt = l = acc = None
        for j in range(n_chunks):
            m_nxt = merge(m_nxt, scores(qq, j, s_nxt), jnp.maximum)
            lj, aj = weights(j, s_cur, m_cur)
            l = merge(l, lj, jnp.add)
            acc = merge(acc, aj, jnp.add)
        finish(t_cur, acc, l)
        return m_nxt

    qq = stacked_q(0)
    m0 = None
    for j in range(n_chunks):
        m0 = merge(m0, scores(qq, j, s0_scr), jnp.maximum)

    def pair(u, m_even):
        t = 2 * u
        m_odd = step(t, t + 1, s0_scr, s1_scr, m_even)
        return step(t + 1, jnp.minimum(t + 2, n_tiles - 1), s1_scr, s0_scr, m_odd)

    lax.fori_loop(0, n_tiles // 2, pair, m0)


def _latent_attn(q, k, vt, cache_k, cache_v, w, *, mode, layer, lam_init, batch, seq, tq):
    heads = KV_GQA if mode == 'gqa' else H_DIFF
    group = GQA_GROUP if mode == 'gqa' else 1
    past = cache_k.shape[2]
    in_specs = [pl.BlockSpec((seq, group * HEAD), lambda b, h: (b, h)),
                pl.BlockSpec((seq, HEAD), lambda b, h: (b, h)),
                pl.BlockSpec((HEAD, seq), lambda b, h: (h, b)),
                pl.BlockSpec((None, None, past, HEAD), lambda b, h: (b, layer, 0, h)),
                pl.BlockSpec((None, None, past, HEAD), lambda b, h: (b, layer, 0, h))]
    args = [q, k, vt, cache_k, cache_v]
    if mode == 'diff':
        in_specs += [pl.BlockSpec((None, 4, DK_DIFF), lambda b, h: (layer, 0, 0)),
                     pl.BlockSpec((None, None, 1, HEAD), lambda b, h: (layer, h, 0, 0))]
        args += [w['diff_lambda'], w['diff_norm_w']]
    m_rows = (GQA_GROUP if mode == 'gqa' else 2) * tq
    return pl.pallas_call(
        functools.partial(_latent_attn_kernel, seq=seq, past=past, tq=tq, mode=mode, lam_init=lam_init),
        grid=(batch, heads),
        in_specs=in_specs,
        out_specs=pl.BlockSpec((seq, group * HEAD), lambda b, h: (b, h)),
        out_shape=jax.ShapeDtypeStruct((batch * seq, heads * group * HEAD), BF16),
        scratch_shapes=[pltpu.VMEM((past, HEAD), BF16), pltpu.VMEM((HEAD, past), BF16),
                        pltpu.VMEM((seq + past, m_rows), F32), pltpu.VMEM((seq + past, m_rows), F32)],
        compiler_params=_params(("parallel", "parallel")),
        name="latent_%s_attention" % mode,
    )(*args)


def _final_norm_kernel(x_ref, w_ref, o_ref):
    x = x_ref[...]
    o_ref[...] = x * lax.rsqrt(jnp.mean(x * x, axis=-1, keepdims=True) + EPS) * w_ref[...]


def _final_norm(x, w, *, tm):
    m = x.shape[0]
    return pl.pallas_call(
        _final_norm_kernel,
        grid=(m // tm,),
        in_specs=[pl.BlockSpec((tm, D_MODEL), lambda i: (i, 0)),
                  pl.BlockSpec((1, D_MODEL), lambda i: (0, 0))],
        out_specs=pl.BlockSpec((tm, D_MODEL), lambda i: (i, 0)),
        out_shape=jax.ShapeDtypeStruct((m, D_MODEL), F32),
        compiler_params=_params(("parallel",)),
        name="final_norm",
    )(x, w)


def _rope_tables(rows, d, width):
    axis = d // 2
    inv = 1.0 / (ROPE_THETA ** (jnp.arange(0, axis, 2, dtype=F32) / axis))
    r = jnp.repeat(jnp.arange(rows, dtype=F32), GRID_W)
    cl = jnp.tile(jnp.arange(GRID_W, dtype=F32), rows)
    ang = jnp.concatenate([r[:, None] * inv, cl[:, None] * inv], axis=-1)
    cos = jnp.repeat(jnp.cos(ang), 2, axis=-1)
    sin = jnp.repeat(jnp.sin(ang), 2, axis=-1) * jnp.tile(jnp.asarray([-1.0, 1.0], F32), d // 2)
    reps = width // d
    return jnp.tile(cos, (1, reps)), jnp.tile(sin, (1, reps))


def _trunk_layer(x, mod, w, *, layer, batch, seq, mod_row0, rows_per_mod, tm, lam_init, ctx):
    dense = dict(layer=layer, rows_per_mod=rows_per_mod, mod_row0=mod_row0, tm=tm)
    h = _ffn_up(x, mod, w, which=0, sub=0, tn=512, **dense)
    x = _ffn_down(h, x, mod, w, which=0, gate_row=2, tn=512, **dense)

    if ctx is None:
        proj = _in_proj(x, mod, w, tn=1024, **dense)
        ret, new_state = _retention(proj, w, None, layer=layer, batch=batch, seq=seq)
        gqa, k_norm = _ctx_gqa(proj, w, layer=layer, batch=batch, seq=seq, tq=128)
        dif = _ctx_diff(proj, w, layer=layer, lam_init=lam_init, batch=batch, seq=seq, tq=256)
        new = (new_state,
               k_norm.reshape(batch, seq, KV_GQA, HEAD),
               proj[:, COL_GV * HEAD:(COL_GV + KV_GQA) * HEAD].reshape(batch, seq, KV_GQA, HEAD),
               proj[:, COL_DK * HEAD:(COL_DK + H_DIFF) * HEAD].reshape(batch, seq, H_DIFF, HEAD),
               proj[:, COL_DV * HEAD:(COL_DV + H_DIFF) * HEAD].reshape(batch, seq, H_DIFF, HEAD))
    else:
        ret_in, gq, gk, gvt, dq, dk, dvt = _latent_in_proj(
            x, mod, w, ctx['rope_g'], ctx['rope_d'], layer=layer, seq=seq, mod_row0=mod_row0, tm=min(512, tm))
        ret, _ = _retention(ret_in, w, ctx['state_ret'], layer=layer, batch=batch, seq=seq)
        attn = dict(layer=layer, lam_init=lam_init, batch=batch, seq=seq)
        gqa = _latent_attn(gq, gk, gvt, ctx['k_gqa'], ctx['v_gqa'], w, mode='gqa', tq=128, **attn)
        dif = _latent_attn(dq, dk, dvt, ctx['k_diff'], ctx['v_diff'], w, mode='diff', tq=256, **attn)
        new = None
    x = _out_proj(ret, gqa, dif, x, mod, w, layer=layer, rows_per_mod=rows_per_mod, mod_row0=mod_row0,
                  tm=min(512, tm))

    h = _ffn_up(x, mod, w, which=1, sub=2, tn=512, **dense)
    x = _ffn_down(h, x, mod, w, which=1, gate_row=8, tn=512, **dense)
    return x, new


def kernel(x_prompt, x_sample, c, state_ret, cache_gqa_k, cache_gqa_v, cache_diff_k, cache_diff_v, c_ctx,
           w_mod, b_mod, norm_w, ffn_w_in, ffn_w_out, w_in, w_out, ret_decay, ret_gn_w, gqa_qk_norm,
           diff_lambda, diff_norm_w, final_norm_w):
    bp, sp, _ = x_prompt.shape
    bs, ss, _ = x_sample.shape
    depth = w_in.shape[0]
    past = cache_gqa_k.shape[2]
    rows = ss // GRID_W
    assert 1 + bs <= N_COND

    cond = jnp.zeros((N_COND, D_MODEL), F32).at[0].set(c_ctx).at[1:1 + bs].set(c)
    w = {'norm_w': norm_w,
         'ffn_w_in': ffn_w_in.astype(BF16), 'ffn_w_out': ffn_w_out.astype(BF16),
         'w_in': w_in.astype(BF16), 'w_out': w_out.astype(BF16),
         'ret_decay': ret_decay.reshape(depth, 2, H_RET, 1, 1),
         'ret_gn_w': ret_gn_w.reshape(depth, H_RET, 1, HEAD),
         'gqa_qk_norm': gqa_qk_norm,
         'diff_lambda': diff_lambda,
         'diff_norm_w': diff_norm_w.reshape(depth, H_DIFF, 1, HEAD)}
    ctx = {'state_ret': state_ret,
           'k_gqa': cache_gqa_k.reshape(bs, depth, past, KV_GQA * HEAD),
           'v_gqa': cache_gqa_v.reshape(bs, depth, past, KV_GQA * HEAD),
           'k_diff': cache_diff_k.reshape(bs, depth, past, H_DIFF * HEAD),
           'v_diff': cache_diff_v.reshape(bs, depth, past, H_DIFF * HEAD),
           'rope_g': _rope_tables(rows, HEAD, HEAD),
           'rope_d': _rope_tables(rows, DK_DIFF, HEAD)}
    b_mod3 = b_mod.reshape(depth, 1, N_MOD * D_MODEL)

    xp = x_prompt.reshape(bp * sp, D_MODEL)
    xs = x_sample.reshape(bs * ss, D_MODEL)
    new_ret, new_gk, new_gv, new_dk, new_dv = [], [], [], [], []
    for l in range(depth):
        lam_init = 0.8 - 0.6 * math.exp(-0.3 * l)
        mod = _modulation(cond, w_mod, b_mod3, l).reshape(N_COND, N_MOD, D_MODEL)
        xp, new = _trunk_layer(xp, mod, w, layer=l, batch=bp, seq=sp, mod_row0=0, rows_per_mod=bp * sp,
                               tm=min(ROW_TILE, bp * sp), lam_init=lam_init, ctx=None)
        new_ret.append(new[0])
        new_gk.append(new[1])
        new_gv.append(new[2])
        new_dk.append(new[3])
        new_dv.append(new[4])
        xs, _ = _trunk_layer(xs, mod, w, layer=l, batch=bs, seq=ss, mod_row0=1, rows_per_mod=ss,
                             tm=min(ROW_TILE, ss), lam_init=lam_init, ctx=ctx)
    fw = final_norm_w.reshape(1, D_MODEL)
    y_prompt = _final_norm(xp, fw, tm=min(ROW_TILE, bp * sp)).reshape(bp, sp, D_MODEL)
    y_sample = _final_norm(xs, fw, tm=min(ROW_TILE, ss)).reshape(bs, ss, D_MODEL)
    return (y_prompt, y_sample, jnp.stack(new_ret, axis=1), jnp.stack(new_gk, axis=1),
            jnp.stack(new_gv, axis=1), jnp.stack(new_dk, axis=1), jnp.stack(new_dv, axis=1))
```
